```python
import jax, jax.numpy as jnp
from jax import lax
import numpy as np

D_MODEL = 1024
BATCH = 8
SEQ = 2048
DEPTH = 4
DEC_BATCH = 32
DEC_SEQ = 2048
PAST_LEN = 128

N_EVEN = (DEPTH + 1) // 2
N_ODD = DEPTH // 2
EPS = 1e-6

SSD_HEADS = 16
SSD_HEAD_DIM = 64
SSD_INNER = SSD_HEADS * SSD_HEAD_DIM
SSD_GROUPS = 2
SSD_HPG = SSD_HEADS // SSD_GROUPS
SSD_STATE = 128
SSD_CONV = 3
SSD_CHUNK = 128
SSD_CONV_DIM = SSD_INNER + 2 * SSD_GROUPS * SSD_STATE

MLA_HEADS = 8
MLA_Q_LORA = 256
MLA_KV_LORA = 128
MLA_NOPE = 128
MLA_ROPE = 64
MLA_V = 128
MLA_QBLOCK = 128
ROPE_THETA = 10000.0

EVEN_IN = SSD_INNER + SSD_CONV_DIM + 2 * SSD_HEADS + MLA_Q_LORA + MLA_KV_LORA + MLA_ROPE
EVEN_MIX = SSD_INNER + MLA_HEADS * MLA_V

GLA_HEADS = 4
GLA_KEY = D_MODEL // 2
GLA_VAL = D_MODEL
GLA_HK = GLA_KEY // GLA_HEADS
GLA_HV = GLA_VAL // GLA_HEADS
GLA_GATE_RANK = 16
GLA_GATE_NORM = 16.0
GLA_CHUNK = 64
ODD_IN = 2 * GLA_KEY + 2 * GLA_VAL + 2 * GLA_GATE_RANK

D_FF = 2816
FFN_CONV = 3

kernel_name = 'hybrid_ssd_mla_gla_encoder'


def rmsnorm(x, w):
    xf = x.astype(jnp.float32)
    y = xf * lax.rsqrt(jnp.mean(xf * xf, axis=-1, keepdims=True) + EPS)
    return (y * w.astype(jnp.float32)).astype(x.dtype)


def dwconv_centred(x, w, b):
    k, c = w.shape
    y = lax.conv_general_dilated(x, w[:, None, :].astype(x.dtype), window_strides=(1,),
                                 padding=[(k // 2, k // 2)],
                                 dimension_numbers=('NWC', 'WIO', 'NWC'),
                                 feature_group_count=c)
    return y + b.astype(x.dtype)


def to_chunks(t, size):
    b, s = t.shape[:2]
    return jnp.moveaxis(t.reshape((b, s // size, size) + t.shape[2:]), 1, 0)


def from_chunks(t):
    t = jnp.moveaxis(t, 0, 1)
    return t.reshape((t.shape[0], t.shape[1] * t.shape[2]) + t.shape[3:])


def flip_seq(t):
    return jnp.flip(t, axis=1)


def ssd_scan(x, dt, a, b_in, c_in):
    bsz, _, g, e, p = x.shape
    n = b_in.shape[-1]
    causal = jnp.tril(jnp.ones((SSD_CHUNK, SSD_CHUNK), dtype=bool))[None, :, :, None, None]

    def step(state, inp):
        xc, dtc, bc, cc = inp
        acum = jnp.cumsum(dtc * a, axis=1)
        diff = acum[:, :, None] - acum[:, None, :]
        decay = jnp.exp(jnp.where(causal, diff, -jnp.inf))
        cb = jnp.einsum('blgn,bsgn->blsg', cc, bc)
        scores = cb[..., None] * decay * dtc[:, None]
        y = jnp.einsum('blsge,bsgep->blgep', scores, xc)
        y = y + jnp.einsum('blgn,bgepn->blgep', cc, state) * jnp.exp(acum)[..., None]
        w_end = jnp.exp(acum[:, -1:] - acum) * dtc
        state = state * jnp.exp(acum[:, -1])[..., None, None] + jnp.einsum('bsge,bsgep,bsgn->bgepn', w_end, xc, bc)
        return state, y

    state0 = jnp.zeros((bsz, g, e, p, n), jnp.float32)
    _, ys = lax.scan(step, state0, (to_chunks(x, SSD_CHUNK), to_chunks(dt, SSD_CHUNK),
                                    to_chunks(b_in, SSD_CHUNK), to_chunks(c_in, SSD_CHUNK)))
    return from_chunks(ys)


def rope_tables(s):
    inv = 1.0 / (ROPE_THETA ** (jnp.arange(0, MLA_ROPE, 2, dtype=jnp.float32) / MLA_ROPE))
    ang = jnp.arange(s, dtype=jnp.float32)[:, None] * inv[None, :]
    return jnp.cos(ang), jnp.sin(ang)


def apply_rope(t, cos, sin):
    t1, t2 = jnp.split(t.astype(jnp.float32), 2, axis=-1)
    return jnp.concatenate([t1 * cos - t2 * sin, t1 * sin + t2 * cos], axis=-1).astype(t.dtype)


def mla_attention(q_a, kv_a, q_norm_w, w_qb, kv_norm_w, w_kvb):
    bsz, s, _ = q_a.shape
    q = jnp.einsum('bsr,rf->bsf', rmsnorm(q_a, q_norm_w), w_qb).reshape(bsz, s, MLA_HEADS, MLA_NOPE + MLA_ROPE)
    q_nope, q_rope = q[..., :MLA_NOPE], q[..., MLA_NOPE:]
    c_kv = rmsnorm(kv_a[..., :MLA_KV_LORA], kv_norm_w)
    k_rope = kv_a[..., MLA_KV_LORA:]
    cos, sin = rope_tables(s)
    q_rope = apply_rope(q_rope, cos[None, :, None], sin[None, :, None])
    k_rope = apply_rope(k_rope, cos[None], sin[None])
    w_kvb = w_kvb.reshape(MLA_KV_LORA, MLA_HEADS, MLA_NOPE + MLA_V)
    w_uk, w_uv = w_kvb[..., :MLA_NOPE], w_kvb[..., MLA_NOPE:]
    q_lat = jnp.einsum('bshd,chd->bshc', q_nope, w_uk)
    scale = (MLA_NOPE + MLA_ROPE) ** -0.5

    def block(qs):
        ql, qr = qs
        sc = jnp.einsum('bqhc,bkc->bhqk', ql, c_kv) + jnp.einsum('bqhr,bkr->bhqk', qr, k_rope)
        pr = jax.nn.softmax(sc.astype(jnp.float32) * scale, axis=-1).astype(c_kv.dtype)
        return jnp.einsum('bhqk,bkc->bqhc', pr, c_kv)

    o_lat = from_chunks(lax.map(block, (to_chunks(q_lat, MLA_QBLOCK), to_chunks(q_rope, MLA_QBLOCK))))
    o = jnp.einsum('bshc,chd->bshd', o_lat, w_uv)
    return o.reshape(bsz, s, MLA_HEADS * MLA_V)


def ssd_mla_mixer(h, w_in, conv_w, conv_b, a_log, dt_bias, d_skip, ssd_norm_w,
                  q_norm_w, w_qb, kv_norm_w, w_kvb, w_out):
    bsz, s, _ = h.shape
    f32 = jnp.float32
    proj = jnp.einsum('bsd,df->bsf', h, w_in)
    o1 = SSD_INNER
    o2 = o1 + SSD_CONV_DIM
    o3 = o2 + 2 * SSD_HEADS
    o4 = o3 + MLA_Q_LORA
    z, xbc, dt_raw, q_a, kv_a = jnp.split(proj, [o1, o2, o3, o4], axis=-1)
    xbc = jax.nn.silu(dwconv_centred(xbc, conv_w, conv_b))
    xs, b_in, c_in = jnp.split(xbc, [SSD_INNER, SSD_INNER + SSD_GROUPS * SSD_STATE], axis=-1)
    xs = xs.reshape(bsz, s, SSD_GROUPS, SSD_HPG, SSD_HEAD_DIM).astype(f32)
    b_in = b_in.reshape(bsz, s, SSD_GROUPS, SSD_STATE).astype(f32)
    c_in = c_in.reshape(bsz, s, SSD_GROUPS, SSD_STATE).astype(f32)
    dt = jax.nn.softplus(dt_raw.astype(f32).reshape(bsz, s, 2, SSD_GROUPS, SSD_HPG)
                         + dt_bias.astype(f32).reshape(2, SSD_GROUPS, SSD_HPG))
    a = -jnp.exp(a_log.astype(f32)).reshape(2, SSD_GROUPS, SSD_HPG)
    y_fwd = ssd_scan(xs, dt[:, :, 0], a[0], b_in, c_in)
    y_bwd = flip_seq(ssd_scan(flip_seq(xs), flip_seq(dt[:, :, 1]), a[1], flip_seq(b_in), flip_seq(c_in)))
    y = y_fwd + y_bwd + xs * d_skip.astype(f32).reshape(SSD_GROUPS, SSD_HPG)[:, :, None]
    gate = jax.nn.silu(z.astype(f32)).reshape(bsz, s, SSD_GROUPS, SSD_INNER // SSD_GROUPS)
    y = y.reshape(bsz, s, SSD_GROUPS, SSD_INNER // SSD_GROUPS) * gate
    y = rmsnorm(y, ssd_norm_w.reshape(SSD_GROUPS, SSD_INNER // SSD_GROUPS)).reshape(bsz, s, SSD_INNER).astype(h.dtype)
    o_mla = mla_attention(q_a, kv_a, q_norm_w, w_qb, kv_norm_w, w_kvb)
    return jnp.einsum('bsf,fd->bsd', jnp.concatenate([y, o_mla], axis=-1), w_out)


def gla_scan(q, k, v, g):
    bsz, _, nh, dk = q.shape
    dv = v.shape[-1]
    causal = jnp.tril(jnp.ones((GLA_CHUNK, GLA_CHUNK), dtype=bool))[None, :, :, None]

    def step(state, inp):
        qc, kc, vc, gc = inp
        gcum = jnp.cumsum(gc, axis=1)
        q_t = qc * jnp.exp(gcum)
        k_t = kc * jnp.exp(-gcum)
        att = jnp.where(causal, jnp.einsum('blhk,bshk->blsh', q_t, k_t), 0.0)
        o = jnp.einsum('blsh,bshv->blhv', att, vc) + jnp.einsum('blhk,bhkv->blhv', q_t, state)
        g_end = gcum[:, -1]
        state = state * jnp.exp(g_end)[..., None] + jnp.einsum('bshk,bshv->bhkv', kc * jnp.exp(g_end[:, None] - gcum), vc)
        return state, o

    state0 = jnp.zeros((bsz, nh, dk, dv), jnp.float32)
    _, os_ = lax.scan(step, state0, (to_chunks(q, GLA_CHUNK), to_chunks(k, GLA_CHUNK),
                                     to_chunks(v, GLA_CHUNK), to_chunks(g, GLA_CHUNK)))
    return from_chunks(os_)


def gla_mixer(h, w_in, w_gk2, b_gk, norm_w, w_out):
    bsz, s, _ = h.shape
    f32 = jnp.float32
    proj = jnp.einsum('bsd,df->bsf', h, w_in)
    q, k, v, g, lr = jnp.split(proj, [GLA_KEY, 2 * GLA_KEY, 2 * GLA_KEY + GLA_VAL, 2 * GLA_KEY + 2 * GLA_VAL], axis=-1)
    lr = lr.reshape(bsz, s, 2, GLA_GATE_RANK).astype(f32)
    gk = jax.nn.log_sigmoid(jnp.einsum('bsdr,drk->bsdk', lr, w_gk2.astype(f32)) + b_gk.astype(f32)) / GLA_GATE_NORM
    gk = gk.reshape(bsz, s, 2, GLA_HEADS, GLA_HK)
    q = q.reshape(bsz, s, GLA_HEADS, GLA_HK).astype(f32) * GLA_HK ** -0.5
    k = k.reshape(bsz, s, GLA_HEADS, GLA_HK).astype(f32)
    v = v.reshape(bsz, s, GLA_HEADS, GLA_HV).astype(f32)
    o = gla_scan(q, k, v, gk[:, :, 0]) + flip_seq(gla_scan(flip_seq(q), flip_seq(k), flip_seq(v), flip_seq(gk[:, :, 1])))
    o = rmsnorm(o, norm_w) * jax.nn.silu(g.reshape(bsz, s, GLA_HEADS, GLA_HV).astype(f32))
    return jnp.einsum('bsf,fd->bsd', o.reshape(bsz, s, GLA_VAL).astype(h.dtype), w_out)


def conv_ffn(h, w_in, conv_w, conv_b, w_out):
    u = dwconv_centred(jnp.einsum('bsd,df->bsf', h, w_in), conv_w, conv_b)
    gate, up = jnp.split(u, 2, axis=-1)
    return jnp.einsum('bsf,fd->bsd', jax.nn.gelu(gate, approximate=True) * up, w_out)


def trunk(x, p):
    for layer in range(DEPTH):
        i = layer // 2
        h = rmsnorm(x, p['norm_w'][layer, 0])
        if layer % 2 == 0:
            m = ssd_mla_mixer(h, p['hyb_w_in'][i], p['ssd_conv_w'][i], p['ssd_conv_b'][i], p['ssd_a_log'][i],
                              p['ssd_dt_bias'][i], p['ssd_d'][i], p['ssd_norm_w'][i], p['mla_q_norm_w'][i],
                              p['mla_w_qb'][i], p['mla_kv_norm_w'][i], p['mla_w_kvb'][i], p['hyb_w_out'][i])
        else:
            m = gla_mixer(h, p['gla_w_in'][i], p['gla_w_gk2'][i], p['gla_b_gk'][i], p['gla_norm_w'][i], p['gla_w_out'][i])
        x = x + rmsnorm(m, p['norm_w'][layer, 1])
        h = rmsnorm(x, p['norm_w'][layer, 2])
        f = conv_ffn(h, p['ffn_w_in'][layer], p['ffn_conv_w'][layer], p['ffn_conv_b'][layer], p['ffn_w_out'][layer])
        x = x + rmsnorm(f, p['norm_w'][layer, 3])
    return x


def setup_inputs(seed: int = 0) -> dict:
    key = jax.random.key(seed)
    ks = iter(jax.random.split(key, 32))
    f32 = jnp.float32

    def w(shape, fan_in):
        return jax.random.normal(next(ks), shape, f32) * fan_in ** -0.5

    def gain(shape):
        return 1.0 + 0.05 * jax.random.normal(next(ks), shape, f32)

    def small(shape):
        return 0.01 * jax.random.normal(next(ks), shape, f32)

    x_prompt = jax.random.normal(next(ks), (BATCH, SEQ, D_MODEL), f32)
    x_sample = jax.random.normal(next(ks), (DEC_BATCH, DEC_SEQ, D_MODEL), f32)
    hyb_w_in = w((N_EVEN, D_MODEL, EVEN_IN), D_MODEL)
    ssd_conv_w = w((N_EVEN, SSD_CONV, SSD_CONV_DIM), SSD_CONV)
    ssd_conv_b = small((N_EVEN, SSD_CONV_DIM))
    ssd_a_log = jnp.log(jax.random.uniform(next(ks), (N_EVEN, 2, SSD_HEADS), f32, 1.0, 16.0))
    dt0 = jnp.exp(jax.random.uniform(next(ks), (N_EVEN, 2, SSD_HEADS), f32, float(np.log(1e-3)), float(np.log(1e-1))))
    ssd_dt_bias = dt0 + jnp.log(-jnp.expm1(-dt0))
    ssd_d = 1.0 + 0.1 * jax.random.normal(next(ks), (N_EVEN, SSD_HEADS), f32)
    ssd_norm_w = gain((N_EVEN, SSD_INNER))
    mla_q_norm_w = gain((N_EVEN, MLA_Q_LORA))
    mla_w_qb = w((N_EVEN, MLA_Q_LORA, MLA_HEADS * (MLA_NOPE + MLA_ROPE)), MLA_Q_LORA)
    mla_kv_norm_w = gain((N_EVEN, MLA_KV_LORA))
    mla_w_kvb = w((N_EVEN, MLA_KV_LORA, MLA_HEADS * (MLA_NOPE + MLA_V)), MLA_KV_LORA)
    hyb_w_out = w((N_EVEN, EVEN_MIX, D_MODEL), EVEN_MIX)
    gla_w_in = w((N_ODD, D_MODEL, ODD_IN), D_MODEL)
    gla_w_gk2 = w((N_ODD, 2, GLA_GATE_RANK, GLA_KEY), GLA_GATE_RANK)
    gla_b_gk = 0.1 * jax.random.normal(next(ks), (N_ODD, 2, GLA_KEY), f32)
    gla_norm_w = gain((N_ODD, GLA_HV))
    gla_w_out = w((N_ODD, GLA_VAL, D_MODEL), GLA_VAL)
    ffn_w_in = w((DEPTH, D_MODEL, 2 * D_FF), D_MODEL)
    ffn_conv_w = w((DEPTH, FFN_CONV, 2 * D_FF), FFN_CONV)
    ffn_conv_b = small((DEPTH, 2 * D_FF))
    ffn_w_out = w((DEPTH, D_FF, D_MODEL), D_FF)
    norm_w = gain((DEPTH, 4, D_MODEL))
    return {'x_prompt': x_prompt, 'x_sample': x_sample,
            'hyb_w_in': hyb_w_in, 'ssd_conv_w': ssd_conv_w, 'ssd_conv_b': ssd_conv_b,
            'ssd_a_log': ssd_a_log, 'ssd_dt_bias': ssd_dt_bias, 'ssd_d': ssd_d, 'ssd_norm_w': ssd_norm_w,
            'mla_q_norm_w': mla_q_norm_w, 'mla_w_qb': mla_w_qb, 'mla_kv_norm_w': mla_kv_norm_w,
            'mla_w_kvb': mla_w_kvb, 'hyb_w_out': hyb_w_out,
            'gla_w_in': gla_w_in, 'gla_w_gk2': gla_w_gk2, 'gla_b_gk': gla_b_gk,
            'gla_norm_w': gla_norm_w, 'gla_w_out': gla_w_out,
            'ffn_w_in': ffn_w_in, 'ffn_conv_w': ffn_conv_w, 'ffn_conv_b': ffn_conv_b, 'ffn_w_out': ffn_w_out,
            'norm_w': norm_w}


def reference(x_prompt, x_sample, hyb_w_in, ssd_conv_w, ssd_conv_b, ssd_a_log, ssd_dt_bias, ssd_d, ssd_norm_w,
              mla_q_norm_w, mla_w_qb, mla_kv_norm_w, mla_w_kvb, hyb_w_out,
              gla_w_in, gla_w_gk2, gla_b_gk, gla_norm_w, gla_w_out,
              ffn_w_in, ffn_conv_w, ffn_conv_b, ffn_w_out, norm_w):
    params = dict(hyb_w_in=hyb_w_in, ssd_conv_w=ssd_conv_w, ssd_conv_b=ssd_conv_b, ssd_a_log=ssd_a_log,
                  ssd_dt_bias=ssd_dt_bias, ssd_d=ssd_d, ssd_norm_w=ssd_norm_w,
                  mla_q_norm_w=mla_q_norm_w, mla_w_qb=mla_w_qb, mla_kv_norm_w=mla_kv_norm_w,
                  mla_w_kvb=mla_w_kvb, hyb_w_out=hyb_w_out,
                  gla_w_in=gla_w_in, gla_w_gk2=gla_w_gk2, gla_b_gk=gla_b_gk, gla_norm_w=gla_norm_w,
                  gla_w_out=gla_w_out, ffn_w_in=ffn_w_in, ffn_conv_w=ffn_conv_w, ffn_conv_b=ffn_conv_b,
                  ffn_w_out=ffn_w_out, norm_w=norm_w)
    y_prompt = trunk(x_prompt, params)
    y_sample = trunk(x_sample, params)
    return (y_prompt, y_sample)
```

```python
import functools
import math

import jax
import jax.numpy as jnp
from jax import lax
from jax.experimental import pallas as pl
from jax.experimental.pallas import tpu as pltpu

F32 = jnp.float32
BF16 = jnp.bfloat16

D_MODEL = 1024
EPS = 1e-6

SSD_HEADS = 16
SSD_HEAD_DIM = 64
SSD_INNER = SSD_HEADS * SSD_HEAD_DIM
SSD_GROUPS = 2
SSD_HPG = SSD_HEADS // SSD_GROUPS
SSD_STATE = 128
SSD_CHUNK = 128
SSD_CONV_DIM = SSD_INNER + 2 * SSD_GROUPS * SSD_STATE
SSD_GROUP_WIDTH = SSD_INNER // SSD_GROUPS

MLA_HEADS = 8
MLA_Q_LORA = 256
MLA_KV_LORA = 128
MLA_NOPE = 128
MLA_ROPE = 64
MLA_V = 128
ROPE_THETA = 10000.0
MLA_QK_WIDTH = 256

GLA_HEADS = 4
GLA_KEY = D_MODEL // 2
GLA_VAL = D_MODEL
GLA_HK = GLA_KEY // GLA_HEADS
GLA_HV = GLA_VAL // GLA_HEADS
GLA_GATE_RANK = 16
GLA_GATE_NORM = 16.0
GLA_CHUNK = 64

D_FF = 2816
FFN_COL_CHUNK = 256

LANES = 128
HALO = 16
VMEM_LIMIT_BYTES = 56 * 1024 * 1024

_E_Z = 0
_E_XBC = _E_Z + SSD_INNER
_E_QA = _E_XBC + SSD_CONV_DIM
_E_CKV = _E_QA + MLA_Q_LORA
_E_KR = _E_CKV + MLA_KV_LORA
_E_KRSW = _E_KR + LANES
_E_DT = _E_KRSW + LANES
_E_END = _E_DT + LANES


def _dot(a, b):
    return jnp.dot(a, b, preferred_element_type=F32)


def _dot_nt(a, b):
    return lax.dot_general(a, b, (((1,), (1,)), ((), ())), preferred_element_type=F32)


def _dot_tn(a, b):
    return lax.dot_general(a, b, (((0,), (0,)), ((), ())), preferred_element_type=F32)


def _dot_exact(a, b):
    return jnp.dot(a, b, preferred_element_type=F32, precision=lax.Precision.HIGHEST)


def _rms(x, w):
    return x * lax.rsqrt(jnp.mean(x * x, axis=-1, keepdims=True) + EPS) * w


def _softplus(x):
    return jnp.maximum(x, 0.0) + jnp.log1p(jnp.exp(-jnp.abs(x)))


def _silu(x):
    return x * jax.nn.sigmoid(x)


def _gelu_tanh(x):
    c = math.sqrt(2.0 / math.pi)
    return x * (0.5 * (1.0 + jnp.tanh(c * (x + 0.044715 * (x * x * x)))))


def _params(n_axes):
    return pltpu.CompilerParams(dimension_semantics=("arbitrary",) * n_axes,
                                vmem_limit_bytes=VMEM_LIMIT_BYTES)


def _const_spec(shape):
    nd = len(shape)
    return pl.BlockSpec(shape, lambda *_: (0,) * nd, pipeline_mode=pl.Buffered(1))


def _halo_specs(tm, seq):
    per = tm // HALO
    last = seq // HALO - 1
    main = pl.BlockSpec((1, tm, D_MODEL), lambda b, j: (b, j, 0))
    prev = pl.BlockSpec((1, HALO, D_MODEL), lambda b, j: (b, jnp.maximum(j * per - 1, 0), 0))
    nxt = pl.BlockSpec((1, HALO, D_MODEL), lambda b, j: (b, jnp.minimum((j + 1) * per, last), 0))
    return main, prev, nxt


def _fill_halo_rows(hcat_ref, xm_ref, xp_ref, xn_ref, nw, j, nt, tm):
    hp = _rms(xp_ref[0], nw)
    hn = _rms(xn_ref[0], nw)
    hcat_ref[0:HALO, :] = jnp.where(j == 0, 0.0, hp).astype(BF16)
    hcat_ref[HALO:HALO + tm, :] = _rms(xm_ref[0], nw).astype(BF16)
    hcat_ref[HALO + tm:, :] = jnp.where(j == nt - 1, 0.0, hn).astype(BF16)


def _conv3(u, cw, tm):
    rows = tm + 2 * HALO
    um1 = pltpu.roll(u, 1, 0)[HALO:HALO + tm]
    up1 = pltpu.roll(u, rows - 1, 0)[HALO:HALO + tm]
    u0 = u[HALO:HALO + tm]
    return um1 * cw[0:1] + u0 * cw[1:2] + up1 * cw[2:3] + cw[3:4]


def _even_in_kernel(xm_ref, xp_ref, xn_ref, nw_ref, wall_ref, cw_ref, dtb_ref, qnw_ref, wq_ref, wuk_ref,
                    kvnw_ref, cos_ref, sin_ref,
                    z_ref, xs_ref, bc_ref, dt_ref, q_ref, k_ref, hcat_ref, *, tm, nt):
    j = pl.program_id(1)
    _fill_halo_rows(hcat_ref, xm_ref, xp_ref, xn_ref, nw_ref[...], j, nt, tm)

    hcat = hcat_ref[...]
    for c0 in range(0, SSD_CONV_DIM, 512):
        u = _dot(hcat, wall_ref[:, _E_XBC + c0:_E_XBC + c0 + 512])
        y = _silu(_conv3(u, cw_ref[:, c0:c0 + 512], tm)).astype(BF16)
        if c0 < SSD_INNER:
            xs_ref[0, :, c0:c0 + 512] = y
        else:
            bc_ref[0] = y

    hm = hcat_ref[HALO:HALO + tm, :]
    z_ref[0] = _dot(hm, wall_ref[:, _E_Z:_E_XBC]).astype(BF16)
    rest = _dot(hm, wall_ref[:, _E_QA:_E_END])
    o = -_E_QA
    qa = rest[:, o + _E_QA:o + _E_CKV]
    ckv = rest[:, o + _E_CKV:o + _E_KR]
    kr = rest[:, o + _E_KR:o + _E_KRSW]
    krsw = rest[:, o + _E_KRSW:o + _E_DT]
    dtr = rest[:, o + _E_DT:o + _E_END]
    dt_ref[0] = _softplus(dtr + dtb_ref[...])

    cos = cos_ref[...]
    sin = sin_ref[...]
    scale = (MLA_NOPE + MLA_ROPE) ** -0.5
    hq = _rms(qa, qnw_ref[...]).astype(BF16)
    nh = MLA_HEADS * LANES
    qall = _dot(hq, wq_ref[...])
    for h in range(MLA_HEADS):
        qn = qall[:, h * LANES:(h + 1) * LANES].astype(BF16)
        ql = _dot(qn, wuk_ref[h])
        qr = qall[:, nh + h * LANES:nh + (h + 1) * LANES] * cos + qall[:, 2 * nh + h * LANES:2 * nh + (h + 1) * LANES] * sin
        q_ref[0, :, h * MLA_QK_WIDTH:h * MLA_QK_WIDTH + LANES] = (ql * scale).astype(BF16)
        q_ref[0, :, h * MLA_QK_WIDTH + LANES:(h + 1) * MLA_QK_WIDTH] = (qr * scale).astype(BF16)
    k_ref[0, :, 0:LANES] = _rms(ckv, kvnw_ref[...]).astype(BF16)
    k_ref[0, :, LANES:2 * LANES] = (kr * cos + krsw * sin).astype(BF16)


def _even_in(x, nw, wall, cw, dtb, qnw, wq, wuk, kvnw, cos, sin, tm):
    bsz, seq, _ = x.shape
    nt = seq // tm
    main, prev, nxt = _halo_specs(tm, seq)

    def tok(width):
        return pl.BlockSpec((1, tm, width), lambda b, j: (b, j, 0))

    def out(width, dtype):
        return jax.ShapeDtypeStruct((bsz, seq, width), dtype)

    rope_spec = pl.BlockSpec((tm, LANES), lambda b, j: (j, 0))
    return pl.pallas_call(
        functools.partial(_even_in_kernel, tm=tm, nt=nt),
        grid=(bsz, nt),
        in_specs=[main, prev, nxt, _const_spec(nw.shape), _const_spec(wall.shape), _const_spec(cw.shape),
                  _const_spec(dtb.shape), _const_spec(qnw.shape), _const_spec(wq.shape), _const_spec(wuk.shape),
                  _const_spec(kvnw.shape), rope_spec, rope_spec],
        out_specs=[tok(SSD_INNER), tok(SSD_INNER), tok(2 * SSD_GROUPS * SSD_STATE), tok(LANES),
                   tok(MLA_HEADS * MLA_QK_WIDTH), tok(MLA_QK_WIDTH)],
        out_shape=[out(SSD_INNER, BF16), out(SSD_INNER, BF16), out(2 * SSD_GROUPS * SSD_STATE, BF16),
                   out(LANES, F32), out(MLA_HEADS * MLA_QK_WIDTH, BF16), out(MLA_QK_WIDTH, BF16)],
        scratch_shapes=[pltpu.VMEM((tm + 2 * HALO, D_MODEL), BF16)],
        compiler_params=_params(2),
        name="even_in",
    )(x, x, x, nw, wall, cw, dtb, qnw, wq, wuk, kvnw, cos, sin)


def _expand_heads(v, expand):
    hi = v.astype(BF16)
    lo = (v - hi.astype(F32)).astype(BF16)
    return _dot(hi, expand) + _dot(lo, expand)


def _ssd_chunk(x, bc, dt, a, st_ref, reverse, lane0):
    L = SSD_CHUNK
    row = lax.broadcasted_iota(jnp.int32, (L, L), 0)
    col = lax.broadcasted_iota(jnp.int32, (L, L), 1)
    tri = (row <= col) if reverse else (row >= col)
    dta = dt * a
    cum = _dot_exact(tri.astype(F32), dta)
    cum_t = cum.T
    dt_t = dt.T
    tot = cum[0:1] if reverse else cum[L - 1:L]

    er = lax.broadcasted_iota(jnp.int32, (LANES, SSD_INNER), 0)
    ec = lax.broadcasted_iota(jnp.int32, (LANES, SSD_INNER), 1)
    head_of_lane = lax.shift_right_logical(ec, int(math.log2(SSD_HEAD_DIM)))
    expand = jnp.where(er - lane0 == head_of_lane, 1.0, 0.0).astype(BF16)
    ecum = _expand_heads(jnp.exp(cum), expand)
    wend = _expand_heads(jnp.exp(tot - cum) * dt, expand)
    etot = _expand_heads(jnp.broadcast_to(jnp.exp(tot), (8, LANES)), expand)[0:1]

    xw = (x.astype(F32) * wend).astype(BF16)
    lane = lax.broadcasted_iota(jnp.int32, (L, LANES), 1)
    low_half = lane < SSD_HEAD_DIM
    gw = SSD_GROUP_WIDTH
    ys = []
    for g in range(SSD_GROUPS):
        bg = bc[:, g * SSD_STATE:(g + 1) * SSD_STATE]
        cg = bc[:, (SSD_GROUPS + g) * SSD_STATE:(SSD_GROUPS + g + 1) * SSD_STATE]
        cb = _dot_nt(cg, bg)
        st = st_ref[:, g * gw:(g + 1) * gw]
        y_inter = _dot(cg, st.astype(BF16))
        for p in range(SSD_HPG // 2):
            lo = g * gw + p * LANES
            x_pair = x[:, lo:lo + LANES]
            pair = []
            for e in (2 * p, 2 * p + 1):
                c = lane0 + g * SSD_HPG + e
                diff = cum[:, c:c + 1] - cum_t[c:c + 1, :]
                decay = jnp.exp(jnp.where(tri, diff, -jnp.inf))
                sc = (cb * decay * dt_t[c:c + 1, :]).astype(BF16)
                pair.append(_dot(sc, x_pair))
            y_intra = jnp.where(low_half, pair[0], pair[1])
            ys.append(y_intra + y_inter[:, p * LANES:(p + 1) * LANES] * ecum[:, lo:lo + LANES])
        st_ref[:, g * gw:(g + 1) * gw] = st * etot[:, g * gw:(g + 1) * gw] + _dot_tn(bg, xw[:, g * gw:(g + 1) * gw])
    return jnp.concatenate(ys, axis=1)


def _ssd_kernel(xf_ref, xb_ref, bcf_ref, bcb_ref, dtf_ref, dtb_ref, alog_ref, yf_ref, yb_ref, stf_ref, stb_ref, *, nck):
    @pl.when(pl.program_id(1) == 0)
    def _():
        stf_ref[...] = jnp.zeros_like(stf_ref)
        stb_ref[...] = jnp.zeros_like(stb_ref)

    a = -jnp.exp(alog_ref[...])

    def body(ci, carry):
        rf = pl.ds(pl.multiple_of(ci * SSD_CHUNK, SSD_CHUNK), SSD_CHUNK)
        rb = pl.ds(pl.multiple_of((nck - 1 - ci) * SSD_CHUNK, SSD_CHUNK), SSD_CHUNK)
        yf_ref[0, rf, :] = _ssd_chunk(xf_ref[0, rf, :], bcf_ref[0, rf, :], dtf_ref[0, rf, :], a, stf_ref,
                                      False, 0).astype(yf_ref.dtype)
        yb_ref[0, rb, :] = _ssd_chunk(xb_ref[0, rb, :], bcb_ref[0, rb, :], dtb_ref[0, rb, :], a, stb_ref,
                                      True, SSD_HEADS).astype(yb_ref.dtype)
        return carry

    lax.fori_loop(0, nck, body, 0)


def _ssd_scan(xs, bc, dt, alog, tb):
    bsz, seq, _ = xs.shape
    nb = seq // tb

    def fwd(width):
        return pl.BlockSpec((1, tb, width), lambda b, j: (b, j, 0))

    def bwd(width):
        return pl.BlockSpec((1, tb, width), lambda b, j: (b, nb - 1 - j, 0))

    out = jax.ShapeDtypeStruct((bsz, seq, SSD_INNER), BF16)
    return pl.pallas_call(
        functools.partial(_ssd_kernel, nck=tb // SSD_CHUNK),
        grid=(bsz, nb),
        in_specs=[fwd(SSD_INNER), bwd(SSD_INNER), fwd(bc.shape[-1]), bwd(bc.shape[-1]), fwd(LANES), bwd(LANES),
                  _const_spec(alog.shape)],
        out_specs=[fwd(SSD_INNER), bwd(SSD_INNER)],
        out_shape=[out, out],
        scratch_shapes=[pltpu.VMEM((SSD_STATE, SSD_INNER), F32), pltpu.VMEM((SSD_STATE, SSD_INNER), F32)],
        compiler_params=_params(2),
        name="ssd_scan",
    )(xs, xs, bc, bc, dt, dt, alog)


def _mla_kernel(q_ref, k_ref, wuv_ref, o_ref):
    k = k_ref[0]
    v = k[:, 0:MLA_KV_LORA]
    for h in range(MLA_HEADS):
        q = q_ref[0, :, h * MLA_QK_WIDTH:(h + 1) * MLA_QK_WIDTH]
        s = _dot_nt(q, k)
        m = jnp.max(s, axis=-1, keepdims=True)
        p = jnp.exp(s - m)
        l = jnp.sum(p, axis=-1, keepdims=True)
        o_lat = _dot(p.astype(BF16), v) / l
        o_ref[0, :, h * MLA_V:(h + 1) * MLA_V] = _dot(o_lat.astype(BF16), wuv_ref[h]).astype(o_ref.dtype)


def _mla_attention(q, k, wuv, tq):
    bsz, seq, _ = q.shape
    return pl.pallas_call(
        _mla_kernel,
        grid=(bsz, seq // tq),
        in_specs=[pl.BlockSpec((1, tq, q.shape[-1]), lambda b, j: (b, j, 0)),
                  pl.BlockSpec((1, seq, k.shape[-1]), lambda b, j: (b, 0, 0)),
                  _const_spec(wuv.shape)],
        out_specs=pl.BlockSpec((1, tq, MLA_HEADS * MLA_V), lambda b, j: (b, j, 0)),
        out_shape=jax.ShapeDtypeStruct((bsz, seq, MLA_HEADS * MLA_V), BF16),
        compiler_params=_params(2),
        name="mla_attention",
    )(q, k, wuv)


def _even_out_kernel(yf_ref, yb_ref, xs_ref, z_ref, om_ref, x_ref, dskip_ref, snw_ref, wo_ref, nw_ref, o_ref):
    y = yf_ref[...].astype(F32) + yb_ref[...].astype(F32) + xs_ref[...].astype(F32) * dskip_ref[...]
    y = y * _silu(z_ref[...].astype(F32))
    snw = snw_ref[...]
    gw = SSD_GROUP_WIDTH
    m = _dot(om_ref[...], wo_ref[SSD_INNER:, :])
    for g in range(SSD_GROUPS):
        yg = _rms(y[:, g * gw:(g + 1) * gw], snw[:, g * gw:(g + 1) * gw]).astype(BF16)
        m = m + _dot(yg, wo_ref[g * gw:(g + 1) * gw, :])
    o_ref[...] = x_ref[...] + _rms(m, nw_ref[...])


def _even_out(yf, yb, xs, z, om, x, dskip, snw, wo, nw, tm):
    t = x.shape[0]
    tok = pl.BlockSpec((tm, D_MODEL), lambda i: (i, 0))
    return pl.pallas_call(
        _even_out_kernel,
        grid=(t // tm,),
        in_specs=[tok, tok, tok, tok, tok, tok, _const_spec(dskip.shape), _const_spec(snw.shape),
                  _const_spec(wo.shape), _const_spec(nw.shape)],
        out_specs=tok,
        out_shape=jax.ShapeDtypeStruct(x.shape, F32),
        compiler_params=_params(1),
        name="even_out",
    )(yf, yb, xs, z, om, x, dskip, snw, wo, nw)


def _gla_in_kernel(x_ref, nw_ref, w_ref, wgk_ref, bgk_ref, q_ref, k_ref, v_ref, g_ref, gk_ref):
    h = _rms(x_ref[...], nw_ref[...]).astype(BF16)
    o1, o2, o3, o4 = GLA_KEY, 2 * GLA_KEY, 2 * GLA_KEY + GLA_VAL, 2 * GLA_KEY + 2 * GLA_VAL
    q_ref[...] = (_dot(h, w_ref[:, 0:o1]) * GLA_HK ** -0.5).astype(BF16)
    k_ref[...] = _dot(h, w_ref[:, o1:o2]).astype(BF16)
    v_ref[...] = _dot(h, w_ref[:, o2:o3]).astype(BF16)
    g_ref[...] = _dot(h, w_ref[:, o3:o4]).astype(BF16)
    lr = _dot(h, w_ref[:, o4:]).astype(BF16)
    pre = _dot(lr, wgk_ref[...]) + bgk_ref[...]
    gk_ref[...] = (jnp.minimum(pre, 0.0) - jnp.log1p(jnp.exp(-jnp.abs(pre)))) / GLA_GATE_NORM


def _gla_in(x, nw, w, wgk, bgk, tm):
    t = x.shape[0]

    def tok(width):
        return pl.BlockSpec((tm, width), lambda i: (i, 0))

    def out(width, dtype):
        return jax.ShapeDtypeStruct((t, width), dtype)

    return pl.pallas_call(
        _gla_in_kernel,
        grid=(t // tm,),
        in_specs=[tok(D_MODEL), _const_spec(nw.shape), _const_spec(w.shape), _const_spec(wgk.shape),
                  _const_spec(bgk.shape)],
        out_specs=[tok(GLA_KEY), tok(GLA_KEY), tok(GLA_VAL), tok(GLA_VAL), tok(2 * GLA_KEY)],
        out_shape=[out(GLA_KEY, BF16), out(GLA_KEY, BF16), out(GLA_VAL, BF16), out(GLA_VAL, BF16),
                   out(2 * GLA_KEY, F32)],
        compiler_params=_params(1),
        name="gla_in",
    )(x, nw, w, wgk, bgk)


def _gla_chunk(q, k, v, g, st_ref, reverse):
    L = GLA_CHUNK
    row = lax.broadcasted_iota(jnp.int32, (L, L), 0)
    col = lax.broadcasted_iota(jnp.int32, (L, L), 1)
    tri = (row <= col) if reverse else (row >= col)
    cum = _dot_exact(tri.astype(F32), g)
    tot = cum[0:1] if reverse else cum[L - 1:L]
    qf = q.astype(F32)
    kf = k.astype(F32)
    qt = (qf * jnp.exp(cum)).astype(BF16)
    kt = (kf * jnp.exp(-cum)).astype(BF16)
    ke = (kf * jnp.exp(tot - cum)).astype(BF16)
    st = st_ref[...]
    stb = st.astype(BF16)
    outs = []
    upds = []
    for h in range(GLA_HEADS):
        ks = slice(h * GLA_HK, (h + 1) * GLA_HK)
        vh = v[:, h * GLA_HV:(h + 1) * GLA_HV]
        att = jnp.where(tri, _dot_nt(qt[:, ks], kt[:, ks]), 0.0).astype(BF16)
        outs.append(_dot(att, vh) + _dot_nt(qt[:, ks], stb[:, ks]))
        upds.append(_dot_tn(vh, ke[:, ks]))
    st_ref[...] = st * jnp.exp(tot) + jnp.concatenate(upds, axis=1)
    return jnp.concatenate(outs, axis=1)


def _gla_kernel(qf_ref, qb_ref, kf_ref, kb_ref, vf_ref, vb_ref, gf_ref, gb_ref, of_ref, ob_ref, stf_ref, stb_ref, *, nck):
    @pl.when(pl.program_id(1) == 0)
    def _():
        stf_ref[...] = jnp.zeros_like(stf_ref)
        stb_ref[...] = jnp.zeros_like(stb_ref)

    def body(ci, carry):
        rf = pl.ds(pl.multiple_of(ci * GLA_CHUNK, GLA_CHUNK), GLA_CHUNK)
        rb = pl.ds(pl.multiple_of((nck - 1 - ci) * GLA_CHUNK, GLA_CHUNK), GLA_CHUNK)
        of_ref[0, rf, :] = _gla_chunk(qf_ref[0, rf, :], kf_ref[0, rf, :], vf_ref[0, rf, :], gf_ref[0, rf, :],
                                      stf_ref, False).astype(of_ref.dtype)
        ob_ref[0, rb, :] = _gla_chunk(qb_ref[0, rb, :], kb_ref[0, rb, :], vb_ref[0, rb, :], gb_ref[0, rb, :],
                                      stb_ref, True).astype(ob_ref.dtype)
        return carry

    lax.fori_loop(0, nck, body, 0)


def _gla_scan(q, k, v, gk, tb):
    bsz, seq, _ = q.shape
    nb = seq // tb

    def fwd(width, cb=0):
        return pl.BlockSpec((1, tb, width), lambda b, j: (b, j, cb))

    def bwd(width, cb=0):
        return pl.BlockSpec((1, tb, width), lambda b, j: (b, nb - 1 - j, cb))

    out = jax.ShapeDtypeStruct((bsz, seq, GLA_VAL), BF16)
    return pl.pallas_call(
        functools.partial(_gla_kernel, nck=tb // GLA_CHUNK),
        grid=(bsz, nb),
        in_specs=[fwd(GLA_KEY), bwd(GLA_KEY), fwd(GLA_KEY), bwd(GLA_KEY), fwd(GLA_VAL), bwd(GLA_VAL),
                  fwd(GLA_KEY, 0), bwd(GLA_KEY, 1)],
        out_specs=[fwd(GLA_VAL), bwd(GLA_VAL)],
        out_shape=[out, out],
        scratch_shapes=[pltpu.VMEM((GLA_HV, GLA_KEY), F32), pltpu.VMEM((GLA_HV, GLA_KEY), F32)],
        compiler_params=_params(2),
        name="gla_scan",
    )(q, q, k, k, v, v, gk, gk)


def _gla_out_kernel(of_ref, ob_ref, g_ref, x_ref, gnw_ref, wo_ref, nw_ref, o_ref):
    o = of_ref[...].astype(F32) + ob_ref[...].astype(F32)
    gate = _silu(g_ref[...].astype(F32))
    gnw = gnw_ref[...]
    m = None
    for h in range(GLA_HEADS):
        hs = slice(h * GLA_HV, (h + 1) * GLA_HV)
        oh = (_rms(o[:, hs], gnw) * gate[:, hs]).astype(BF16)
        d = _dot(oh, wo_ref[hs, :])
        m = d if m is None else m + d
    o_ref[...] = x_ref[...] + _rms(m, nw_ref[...])


def _gla_out(of, ob, g, x, gnw, wo, nw, tm):
    t = x.shape[0]
    tok = pl.BlockSpec((tm, D_MODEL), lambda i: (i, 0))
    return pl.pallas_call(
        _gla_out_kernel,
        grid=(t // tm,),
        in_specs=[tok, tok, tok, tok, _const_spec(gnw.shape), _const_spec(wo.shape), _const_spec(nw.shape)],
        out_specs=tok,
        out_shape=jax.ShapeDtypeStruct(x.shape, F32),
        compiler_params=_params(1),
        name="gla_out",
    )(of, ob, g, x, gnw, wo, nw)


def _ffn_kernel(xm_ref, xp_ref, xn_ref, nw_ref, wg_ref, wu_ref, cwg_ref, cwu_ref, wo_ref, nwo_ref, o_ref,
                hcat_ref, acc_ref, *, tm, nt, nchunk):
    j = pl.program_id(1)
    _fill_halo_rows(hcat_ref, xm_ref, xp_ref, xn_ref, nw_ref[...], j, nt, tm)
    acc_ref[...] = jnp.zeros_like(acc_ref)

    def body(c, carry):
        hcat = hcat_ref[...]
        gate = _conv3(_dot(hcat, wg_ref[c]), cwg_ref[c], tm)
        up = _conv3(_dot(hcat, wu_ref[c]), cwu_ref[c], tm)
        act = (_gelu_tanh(gate) * up).astype(BF16)
        acc_ref[...] += _dot(act, wo_ref[c])
        return carry

    lax.fori_loop(0, nchunk, body, 0)
    o_ref[0] = xm_ref[0] + _rms(acc_ref[...], nwo_ref[...])


def _ffn(x, nw, wg, wu, cwg, cwu, wo, nwo, tm):
    bsz, seq, _ = x.shape
    nt = seq // tm
    main, prev, nxt = _halo_specs(tm, seq)
    return pl.pallas_call(
        functools.partial(_ffn_kernel, tm=tm, nt=nt, nchunk=wg.shape[0]),
        grid=(bsz, nt),
        in_specs=[main, prev, nxt, _const_spec(nw.shape), _const_spec(wg.shape), _const_spec(wu.shape),
                  _const_spec(cwg.shape), _const_spec(cwu.shape), _const_spec(wo.shape), _const_spec(nwo.shape)],
        out_specs=pl.BlockSpec((1, tm, D_MODEL), lambda b, j: (b, j, 0)),
        out_shape=jax.ShapeDtypeStruct(x.shape, F32),
        scratch_shapes=[pltpu.VMEM((tm + 2 * HALO, D_MODEL), BF16), pltpu.VMEM((tm, D_MODEL), F32)],
        compiler_params=_params(2),
        name="conv_ffn",
    )(x, x, x, nw, wg, wu, cwg, cwu, wo, nwo)


def _pad_cols(w, width):
    return jnp.pad(w, ((0, 0),) * (w.ndim - 1) + ((0, width - w.shape[-1]),))


def _swap_halves(w):
    half = w.shape[-1] // 2
    return jnp.concatenate([w[..., half:], w[..., :half]], axis=-1)


def _row(v):
    return v.reshape(1, -1).astype(F32)


def _prep_even(w_in, conv_w, conv_b, a_log, dt_bias, d_skip, ssd_norm_w, q_norm_w, w_qb, kv_norm_w, w_kvb, w_out):
    o1 = SSD_INNER
    o2 = o1 + SSD_CONV_DIM
    o3 = o2 + 2 * SSD_HEADS
    o4 = o3 + MLA_Q_LORA
    o5 = o4 + MLA_KV_LORA
    w_kr = w_in[:, o5:]
    wall = jnp.concatenate([w_in[:, :o2], w_in[:, o3:o5], _pad_cols(w_kr, LANES), _pad_cols(_swap_halves(w_kr), LANES),
                            _pad_cols(w_in[:, o2:o3], LANES)], axis=1).astype(BF16)
    cw = jnp.concatenate([conv_w, conv_b[None, :]], axis=0).astype(F32)
    dtb = _pad_cols(_row(dt_bias), LANES)
    alog = _pad_cols(_row(a_log), LANES)
    dskip = _row(jnp.repeat(d_skip, SSD_HEAD_DIM))
    wq3 = w_qb.reshape(MLA_Q_LORA, MLA_HEADS, MLA_NOPE + MLA_ROPE)
    rope = wq3[..., MLA_NOPE:]
    wq = jnp.concatenate([wq3[..., :MLA_NOPE].reshape(MLA_Q_LORA, -1),
                          _pad_cols(rope, LANES).reshape(MLA_Q_LORA, -1),
                          _pad_cols(_swap_halves(rope), LANES).reshape(MLA_Q_LORA, -1)], axis=1).astype(BF16)
    wkv3 = w_kvb.reshape(MLA_KV_LORA, MLA_HEADS, MLA_NOPE + MLA_V)
    wuk = jnp.transpose(wkv3[..., :MLA_NOPE], (1, 2, 0)).astype(BF16)
    wuv = jnp.transpose(wkv3[..., MLA_NOPE:], (1, 0, 2)).astype(BF16)
    return dict(wall=wall, cw=cw, dtb=dtb, alog=alog, dskip=dskip, snw=_row(ssd_norm_w), qnw=_row(q_norm_w), wq=wq,
                wuk=wuk, kvnw=_row(kv_norm_w), wuv=wuv, wo=w_out.astype(BF16))


def _rope_tables(seq):
    inv = 1.0 / (ROPE_THETA ** (jnp.arange(0, MLA_ROPE, 2, dtype=F32) / MLA_ROPE))
    ang = jnp.arange(seq, dtype=F32)[:, None] * inv[None, :]
    cos, sin = jnp.cos(ang), jnp.sin(ang)
    return (_pad_cols(jnp.concatenate([cos, cos], axis=1), LANES),
            _pad_cols(jnp.concatenate([-sin, sin], axis=1), LANES))


def _prep_gla(w_in, w_gk2, b_gk, norm_w, w_out):
    w = _pad_cols(w_in, 2 * GLA_KEY + 2 * GLA_VAL + LANES).astype(BF16)
    wgk = jnp.zeros((LANES, 2 * GLA_KEY), F32)
    for d in range(2):
        wgk = wgk.at[d * GLA_GATE_RANK:(d + 1) * GLA_GATE_RANK, d * GLA_KEY:(d + 1) * GLA_KEY].set(w_gk2[d])
    return dict(w=w, wgk=wgk.astype(BF16), bgk=_row(b_gk), gnw=_row(norm_w), wo=w_out.astype(BF16))


def _prep_ffn(w_in, conv_w, conv_b, w_out):
    nchunk = D_FF // FFN_COL_CHUNK

    def cols(w):
        return jnp.transpose(w.reshape(w.shape[0], nchunk, FFN_COL_CHUNK), (1, 0, 2))

    cw = jnp.concatenate([conv_w, conv_b[None, :], jnp.zeros((4, 2 * D_FF), F32)], axis=0).astype(F32)
    return dict(wg=cols(w_in[:, :D_FF]).astype(BF16), wu=cols(w_in[:, D_FF:]).astype(BF16),
                cwg=cols(cw[:, :D_FF]), cwu=cols(cw[:, D_FF:]),
                wo=w_out.reshape(nchunk, FFN_COL_CHUNK, D_MODEL).astype(BF16))


def _tile(total, pref):
    return min(total, pref)


def _trunk(x, p, tiles=None):
    tiles = dict(dict(even_in=512, ssd=512, mla=256, tok=512, gla=256, ffn=512), **(tiles or {}))
    bsz, seq, d = x.shape
    depth = p['norm_w'].shape[0]
    cos, sin = _rope_tables(seq)
    flat = lambda t: t.reshape(bsz * seq, t.shape[-1])
    for layer in range(depth):
        i = layer // 2
        nw = p['norm_w'][layer].astype(F32)
        if layer % 2 == 0:
            e = _prep_even(p['hyb_w_in'][i], p['ssd_conv_w'][i], p['ssd_conv_b'][i], p['ssd_a_log'][i],
                           p['ssd_dt_bias'][i], p['ssd_d'][i], p['ssd_norm_w'][i], p['mla_q_norm_w'][i],
                           p['mla_w_qb'][i], p['mla_kv_norm_w'][i], p['mla_w_kvb'][i], p['hyb_w_out'][i])
            z, xs, bc, dt, q, k = _even_in(x, nw[0:1], e['wall'], e['cw'], e['dtb'], e['qnw'], e['wq'], e['wuk'],
                                           e['kvnw'], cos, sin, _tile(seq, tiles['even_in']))
            yf, yb = _ssd_scan(xs, bc, dt, e['alog'], _tile(seq, tiles['ssd']))
            om = _mla_attention(q, k, e['wuv'], _tile(seq, tiles['mla']))
            x = _even_out(flat(yf), flat(yb), flat(xs), flat(z), flat(om), flat(x), e['dskip'], e['snw'], e['wo'],
                          nw[1:2], _tile(bsz * seq, tiles['tok'])).reshape(bsz, seq, d)
        else:
            gp = _prep_gla(p['gla_w_in'][i], p['gla_w_gk2'][i], p['gla_b_gk'][i], p['gla_norm_w'][i], p['gla_w_out'][i])
            q, k, v, g, gk = _gla_in(flat(x), nw[0:1], gp['w'], gp['wgk'], gp['bgk'], _tile(bsz * seq, tiles['tok']))
            r3 = lambda t: t.reshape(bsz, seq, t.shape[-1])
            of, ob = _gla_scan(r3(q), r3(k), r3(v), r3(gk), _tile(seq, tiles['gla']))
            x = _gla_out(flat(of), flat(ob), g, flat(x), gp['gnw'], gp['wo'], nw[1:2],
                         _tile(bsz * seq, tiles['tok'])).reshape(bsz, seq, d)
        f = _prep_ffn(p['ffn_w_in'][layer], p['ffn_conv_w'][layer], p['ffn_conv_b'][layer], p['ffn_w_out'][layer])
        x = _ffn(x, nw[2:3], f['wg'], f['wu'], f['cwg'], f['cwu'], f['wo'], nw[3:4], _tile(seq, tiles['ffn']))
    return x


def kernel(x_prompt, x_sample, hyb_w_in, ssd_conv_w, ssd_conv_b, ssd_a_log, ssd_dt_bias, ssd_d, ssd_norm_w, mla_q_norm_w, mla_w_qb, mla_kv_norm_w, mla_w_kvb, hyb_w_out, gla_w_in, gla_w_gk2, gla_b_gk, gla_norm_w, gla_w_out, ffn_w_in, ffn_conv_w, ffn_conv_b, ffn_w_out, norm_w):
    params = dict(hyb_w_in=hyb_w_in, ssd_conv_w=ssd_conv_w, ssd_conv_b=ssd_conv_b, ssd_a_log=ssd_a_log,
                  ssd_dt_bias=ssd_dt_bias, ssd_d=ssd_d, ssd_norm_w=ssd_norm_w,
                  mla_q_norm_w=mla_q_norm_w, mla_w_qb=mla_w_qb, mla_kv_norm_w=mla_kv_norm_w,
                  mla_w_kvb=mla_w_kvb, hyb_w_out=hyb_w_out,
                  gla_w_in=gla_w_in, gla_w_gk2=gla_w_gk2, gla_b_gk=gla_b_gk, gla_norm_w=gla_norm_w,
                  gla_w_out=gla_w_out, ffn_w_in=ffn_w_in, ffn_conv_w=ffn_conv_w, ffn_conv_b=ffn_conv_b,
                  ffn_w_out=ffn_w_out, norm_w=norm_w)
    assert x_prompt.shape[1:] == x_sample.shape[1:]
    nb = x_prompt.shape[0]
    y = _trunk(jnp.concatenate([x_prompt, x_sample], axis=0), params)
    return (y[:nb], y[nb:])
```

```python
import functools
import math

import jax
import jax.numpy as jnp
from jax import lax
from jax.experimental import pallas as pl
from jax.experimental.pallas import tpu as pltpu

F32 = jnp.float32
BF16 = jnp.bfloat16

D_MODEL = 1024
EPS = 1e-6

SSD_HEADS = 16
SSD_HEAD_DIM = 64
SSD_INNER = SSD_HEADS * SSD_HEAD_DIM
SSD_GROUPS = 2
SSD_HPG = SSD_HEADS // SSD_GROUPS
SSD_STATE = 128
SSD_CHUNK = 128
SSD_CONV_DIM = SSD_INNER + 2 * SSD_GROUPS * SSD_STATE
SSD_GROUP_WIDTH = SSD_INNER // SSD_GROUPS

MLA_HEADS = 8
MLA_Q_LORA = 256
MLA_KV_LORA = 128
MLA_NOPE = 128
MLA_ROPE = 64
MLA_V = 128
ROPE_THETA = 10000.0
MLA_QK_WIDTH = 256

GLA_HEADS = 4
GLA_KEY = D_MODEL // 2
GLA_VAL = D_MODEL
GLA_HK = GLA_KEY // GLA_HEADS
GLA_HV = GLA_VAL // GLA_HEADS
GLA_GATE_RANK = 16
GLA_GATE_NORM = 16.0
GLA_CHUNK = 64

D_FF = 2816
FFN_COL_CHUNK = 256

LANES = 128
HALO = 16
VMEM_LIMIT_BYTES = 56 * 1024 * 1024

_E_Z = 0
_E_XBC = _E_Z + SSD_INNER
_E_QA = _E_XBC + SSD_CONV_DIM
_E_CKV = _E_QA + MLA_Q_LORA
_E_KR = _E_CKV + MLA_KV_LORA
_E_KRSW = _E_KR + LANES
_E_DT = _E_KRSW + LANES
_E_END = _E_DT + LANES


def _dot(a, b):
    return jnp.dot(a, b, preferred_element_type=F32)


def _dot_nt(a, b):
    return lax.dot_general(a, b, (((1,), (1,)), ((), ())), preferred_element_type=F32)


def _dot_tn(a, b):
    return lax.dot_general(a, b, (((0,), (0,)), ((), ())), preferred_element_type=F32)


def _mask_dot(mask, v):
    t = mask.astype(BF16)
    h1 = v.astype(BF16)
    r1 = v - h1.astype(F32)
    h2 = r1.astype(BF16)
    h3 = (r1 - h2.astype(F32)).astype(BF16)
    return _dot(t, h1) + _dot(t, h2) + _dot(t, h3)


def _rms(x, w):
    return x * lax.rsqrt(jnp.mean(x * x, axis=-1, keepdims=True) + EPS) * w


def _softplus(x):
    return jnp.maximum(x, 0.0) + jnp.log1p(jnp.exp(-jnp.abs(x)))


def _silu(x):
    return x * jax.nn.sigmoid(x)


def _gelu_tanh(x):
    c = math.sqrt(2.0 / math.pi)
    return x * (0.5 * (1.0 + jnp.tanh(c * (x + 0.044715 * (x * x * x)))))


def _params(n_axes, flags=None):
    return pltpu.CompilerParams(dimension_semantics=("arbitrary",) * n_axes,
                                vmem_limit_bytes=VMEM_LIMIT_BYTES, flags=flags)


def _const_spec(shape):
    nd = len(shape)
    return pl.BlockSpec(shape, lambda *_: (0,) * nd, pipeline_mode=pl.Buffered(1))


def _halo_specs(tm, seq):
    per = tm // HALO
    last = seq // HALO - 1
    main = pl.BlockSpec((1, tm, D_MODEL), lambda b, j: (b, j, 0))
    prev = pl.BlockSpec((1, HALO, D_MODEL), lambda b, j: (b, jnp.maximum(j * per - 1, 0), 0))
    nxt = pl.BlockSpec((1, HALO, D_MODEL), lambda b, j: (b, jnp.minimum((j + 1) * per, last), 0))
    return main, prev, nxt


def _fill_halo_rows(hcat_ref, xm_ref, xp_ref, xn_ref, nw, j, nt, tm):
    hp = _rms(xp_ref[0], nw)
    hn = _rms(xn_ref[0], nw)
    hcat_ref[0:HALO, :] = jnp.where(j == 0, 0.0, hp).astype(BF16)
    hcat_ref[HALO:HALO + tm, :] = _rms(xm_ref[0], nw).astype(BF16)
    hcat_ref[HALO + tm:, :] = jnp.where(j == nt - 1, 0.0, hn).astype(BF16)


def _conv3(u, cw, tm):
    rows = tm + 2 * HALO
    um1 = pltpu.roll(u, 1, 0)[HALO:HALO + tm]
    up1 = pltpu.roll(u, rows - 1, 0)[HALO:HALO + tm]
    u0 = u[HALO:HALO + tm]
    return um1 * cw[0:1] + u0 * cw[1:2] + up1 * cw[2:3] + cw[3:4]


def _even_in_kernel(xm_ref, xp_ref, xn_ref, nw_ref, wall_ref, cw_ref, dtb_ref, qnw_ref, wq_ref, wuk_ref,
                    kvnw_ref, cos_ref, sin_ref,
                    z_ref, xs_ref, bc_ref, dt_ref, q_ref, k_ref, hcat_ref, *, tm, nt):
    j = pl.program_id(1)
    _fill_halo_rows(hcat_ref, xm_ref, xp_ref, xn_ref, nw_ref[...], j, nt, tm)

    hcat = hcat_ref[...]
    for c0 in range(0, SSD_CONV_DIM, 512):
        u = _dot(hcat, wall_ref[:, _E_XBC + c0:_E_XBC + c0 + 512])
        y = _silu(_conv3(u, cw_ref[:, c0:c0 + 512], tm)).astype(BF16)
        if c0 < SSD_INNER:
            xs_ref[0, :, c0:c0 + 512] = y
        else:
            bc_ref[0] = y

    hm = hcat_ref[HALO:HALO + tm, :]
    z_ref[0] = _dot(hm, wall_ref[:, _E_Z:_E_XBC]).astype(BF16)
    rest = _dot(hm, wall_ref[:, _E_QA:_E_END])
    o = -_E_QA
    qa = rest[:, o + _E_QA:o + _E_CKV]
    ckv = rest[:, o + _E_CKV:o + _E_KR]
    kr = rest[:, o + _E_KR:o + _E_KRSW]
    krsw = rest[:, o + _E_KRSW:o + _E_DT]
    dtr = rest[:, o + _E_DT:o + _E_END]
    dt_ref[0] = _softplus(dtr + dtb_ref[...])

    cos = cos_ref[...]
    sin = sin_ref[...]
    scale = (MLA_NOPE + MLA_ROPE) ** -0.5
    hq = _rms(qa, qnw_ref[...]).astype(BF16)
    nh = MLA_HEADS * LANES
    qall = _dot(hq, wq_ref[...])
    for h in range(MLA_HEADS):
        qn = qall[:, h * LANES:(h + 1) * LANES].astype(BF16)
        ql = _dot(qn, wuk_ref[h])
        qr = qall[:, nh + h * LANES:nh + (h + 1) * LANES] * cos + qall[:, 2 * nh + h * LANES:2 * nh + (h + 1) * LANES] * sin
        q_ref[0, :, h * MLA_QK_WIDTH:h * MLA_QK_WIDTH + LANES] = (ql * scale).astype(BF16)
        q_ref[0, :, h * MLA_QK_WIDTH + LANES:(h + 1) * MLA_QK_WIDTH] = (qr * scale).astype(BF16)
    k_ref[0, :, 0:LANES] = _rms(ckv, kvnw_ref[...]).astype(BF16)
    k_ref[0, :, LANES:2 * LANES] = (kr * cos + krsw * sin).astype(BF16)


def _even_in(x, nw, wall, cw, dtb, qnw, wq, wuk, kvnw, cos, sin, tm):
    bsz, seq, _ = x.shape
    nt = seq // tm
    main, prev, nxt = _halo_specs(tm, seq)

    def tok(width):
        return pl.BlockSpec((1, tm, width), lambda b, j: (b, j, 0))

    def out(width, dtype):
        return jax.ShapeDtypeStruct((bsz, seq, width), dtype)

    rope_spec = pl.BlockSpec((tm, LANES), lambda b, j: (j, 0))
    return pl.pallas_call(
        functools.partial(_even_in_kernel, tm=tm, nt=nt),
        grid=(bsz, nt),
        in_specs=[main, prev, nxt, _const_spec(nw.shape), _const_spec(wall.shape), _const_spec(cw.shape),
                  _const_spec(dtb.shape), _const_spec(qnw.shape), _const_spec(wq.shape), _const_spec(wuk.shape),
                  _const_spec(kvnw.shape), rope_spec, rope_spec],
        out_specs=[tok(SSD_INNER), tok(SSD_INNER), tok(2 * SSD_GROUPS * SSD_STATE), tok(LANES),
                   tok(MLA_HEADS * MLA_QK_WIDTH), tok(MLA_QK_WIDTH)],
        out_shape=[out(SSD_INNER, BF16), out(SSD_INNER, BF16), out(2 * SSD_GROUPS * SSD_STATE, BF16),
                   out(LANES, F32), out(MLA_HEADS * MLA_QK_WIDTH, BF16), out(MLA_QK_WIDTH, BF16)],
        scratch_shapes=[pltpu.VMEM((tm + 2 * HALO, D_MODEL), BF16)],
        compiler_params=_params(2),
        name="even_in",
    )(x, x, x, nw, wall, cw, dtb, qnw, wq, wuk, kvnw, cos, sin)


def _expand_heads(v, expand):
    hi = v.astype(BF16)
    lo = (v - hi.astype(F32)).astype(BF16)
    return _dot(hi, expand) + _dot(lo, expand)


def _ssd_chunk(x, bc, dt, a, st_ref, reverse, lane0):
    L = SSD_CHUNK
    row = lax.broadcasted_iota(jnp.int32, (L, L), 0)
    col = lax.broadcasted_iota(jnp.int32, (L, L), 1)
    tri = (row <= col) if reverse else (row >= col)
    dta = dt * a
    cum = _mask_dot(tri, dta)
    cum_t = cum.T
    dt_t = dt.T
    tot = cum[0:1] if reverse else cum[L - 1:L]

    er = lax.broadcasted_iota(jnp.int32, (LANES, SSD_INNER), 0)
    ec = lax.broadcasted_iota(jnp.int32, (LANES, SSD_INNER), 1)
    head_of_lane = lax.shift_right_logical(ec, int(math.log2(SSD_HEAD_DIM)))
    expand = jnp.where(er - lane0 == head_of_lane, 1.0, 0.0).astype(BF16)
    ecum = _expand_heads(jnp.exp(cum), expand)
    wend = _expand_heads(jnp.exp(tot - cum) * dt, expand)
    etot = _expand_heads(jnp.broadcast_to(jnp.exp(tot), (8, LANES)), expand)[0:1]

    xw = (x.astype(F32) * wend).astype(BF16)
    lane = lax.broadcasted_iota(jnp.int32, (L, LANES), 1)
    low_half = lane < SSD_HEAD_DIM
    gw = SSD_GROUP_WIDTH
    ys = []
    for g in range(SSD_GROUPS):
        bg = bc[:, g * SSD_STATE:(g + 1) * SSD_STATE]
        cg = bc[:, (SSD_GROUPS + g) * SSD_STATE:(SSD_GROUPS + g + 1) * SSD_STATE]
        cb = _dot_nt(cg, bg)
        st = st_ref[:, g * gw:(g + 1) * gw]
        y_inter = _dot(cg, st.astype(BF16))
        for p in range(SSD_HPG // 2):
            lo = g * gw + p * LANES
            x_pair = x[:, lo:lo + LANES]
            zero = jnp.zeros_like(x_pair)
            x_stack = jnp.concatenate([jnp.where(low_half, x_pair, zero), jnp.where(low_half, zero, x_pair)], axis=0)
            pair = []
            for e in (2 * p, 2 * p + 1):
                c = lane0 + g * SSD_HPG + e
                diff = cum[:, c:c + 1] - cum_t[c:c + 1, :]
                decay = jnp.exp(jnp.where(tri, diff, -jnp.inf))
                pair.append((cb * decay * dt_t[c:c + 1, :]).astype(BF16))
            y_intra = _dot(jnp.concatenate(pair, axis=1), x_stack)
            ys.append(y_intra + y_inter[:, p * LANES:(p + 1) * LANES] * ecum[:, lo:lo + LANES])
        st_ref[:, g * gw:(g + 1) * gw] = st * etot[:, g * gw:(g + 1) * gw] + _dot_tn(bg, xw[:, g * gw:(g + 1) * gw])
    return jnp.concatenate(ys, axis=1)


def _ssd_kernel(xf_ref, xb_ref, bcf_ref, bcb_ref, dtf_ref, dtb_ref, alog_ref, yf_ref, yb_ref, stf_ref, stb_ref, *, nck):
    @pl.when(pl.program_id(1) == 0)
    def _():
        stf_ref[...] = jnp.zeros_like(stf_ref)
        stb_ref[...] = jnp.zeros_like(stb_ref)

    a = -jnp.exp(alog_ref[...])

    def body(ci, carry):
        rf = pl.ds(pl.multiple_of(ci * SSD_CHUNK, SSD_CHUNK), SSD_CHUNK)
        rb = pl.ds(pl.multiple_of((nck - 1 - ci) * SSD_CHUNK, SSD_CHUNK), SSD_CHUNK)
        yf_ref[0, rf, :] = _ssd_chunk(xf_ref[0, rf, :], bcf_ref[0, rf, :], dtf_ref[0, rf, :], a, stf_ref,
                                      False, 0).astype(yf_ref.dtype)
        yb_ref[0, rb, :] = _ssd_chunk(xb_ref[0, rb, :], bcb_ref[0, rb, :], dtb_ref[0, rb, :], a, stb_ref,
                                      True, SSD_HEADS).astype(yb_ref.dtype)
        return carry

    lax.fori_loop(0, nck, body, 0, unroll=True)


def _ssd_scan(xs, bc, dt, alog, tb):
    bsz, seq, _ = xs.shape
    nb = seq // tb

    def fwd(width):
        return pl.BlockSpec((1, tb, width), lambda b, j: (b, j, 0))

    def bwd(width):
        return pl.BlockSpec((1, tb, width), lambda b, j: (b, nb - 1 - j, 0))

    out = jax.ShapeDtypeStruct((bsz, seq, SSD_INNER), BF16)
    return pl.pallas_call(
        functools.partial(_ssd_kernel, nck=tb // SSD_CHUNK),
        grid=(bsz, nb),
        in_specs=[fwd(SSD_INNER), bwd(SSD_INNER), fwd(bc.shape[-1]), bwd(bc.shape[-1]), fwd(LANES), bwd(LANES),
                  _const_spec(alog.shape)],
        out_specs=[fwd(SSD_INNER), bwd(SSD_INNER)],
        out_shape=[out, out],
        scratch_shapes=[pltpu.VMEM((SSD_STATE, SSD_INNER), F32), pltpu.VMEM((SSD_STATE, SSD_INNER), F32)],
        compiler_params=_params(2),
        name="ssd_scan",
    )(xs, xs, bc, bc, dt, dt, alog)


def _mla_kernel(q_ref, k_ref, wuv_ref, o_ref):
    k = k_ref[0]
    v = k[:, 0:MLA_KV_LORA]

    def scores(h):
        return _dot_nt(q_ref[0, :, h * MLA_QK_WIDTH:(h + 1) * MLA_QK_WIDTH], k)

    s_next = scores(0)
    for h in range(MLA_HEADS):
        s = s_next
        if h + 1 < MLA_HEADS:
            s_next = scores(h + 1)
        m = jnp.max(s, axis=-1, keepdims=True)
        p = jnp.exp(s - m)
        l = jnp.sum(p, axis=-1, keepdims=True)
        o_lat = _dot(p.astype(BF16), v) / l
        o_ref[0, :, h * MLA_V:(h + 1) * MLA_V] = _dot(o_lat.astype(BF16), wuv_ref[h]).astype(o_ref.dtype)


def _mla_attention(q, k, wuv, tq):
    bsz, seq, _ = q.shape
    return pl.pallas_call(
        _mla_kernel,
        grid=(bsz, seq // tq),
        in_specs=[pl.BlockSpec((1, tq, q.shape[-1]), lambda b, j: (b, j, 0)),
                  pl.BlockSpec((1, seq, k.shape[-1]), lambda b, j: (b, 0, 0)),
                  _const_spec(wuv.shape)],
        out_specs=pl.BlockSpec((1, tq, MLA_HEADS * MLA_V), lambda b, j: (b, j, 0)),
        out_shape=jax.ShapeDtypeStruct((bsz, seq, MLA_HEADS * MLA_V), BF16),
        compiler_params=_params(2),
        name="mla_attention",
    )(q, k, wuv)


def _even_out_kernel(yf_ref, yb_ref, xs_ref, z_ref, om_ref, x_ref, dskip_ref, snw_ref, wo_ref, nw_ref, o_ref):
    y = yf_ref[...].astype(F32) + yb_ref[...].astype(F32) + xs_ref[...].astype(F32) * dskip_ref[...]
    y = y * _silu(z_ref[...].astype(F32))
    snw = snw_ref[...]
    gw = SSD_GROUP_WIDTH
    m = _dot(om_ref[...], wo_ref[SSD_INNER:, :])
    for g in range(SSD_GROUPS):
        yg = _rms(y[:, g * gw:(g + 1) * gw], snw[:, g * gw:(g + 1) * gw]).astype(BF16)
        m = m + _dot(yg, wo_ref[g * gw:(g + 1) * gw, :])
    o_ref[...] = x_ref[...] + _rms(m, nw_ref[...])


def _even_out(yf, yb, xs, z, om, x, dskip, snw, wo, nw, tm):
    t = x.shape[0]
    tok = pl.BlockSpec((tm, D_MODEL), lambda i: (i, 0))
    return pl.pallas_call(
        _even_out_kernel,
        grid=(t // tm,),
        in_specs=[tok, tok, tok, tok, tok, tok, _const_spec(dskip.shape), _const_spec(snw.shape),
                  _const_spec(wo.shape), _const_spec(nw.shape)],
        out_specs=tok,
        out_shape=jax.ShapeDtypeStruct(x.shape, F32),
        compiler_params=_params(1),
        name="even_out",
    )(yf, yb, xs, z, om, x, dskip, snw, wo, nw)


def _gla_in_kernel(x_ref, nw_ref, w_ref, wgk_ref, bgk_ref, q_ref, k_ref, v_ref, g_ref, gk_ref):
    h = _rms(x_ref[...], nw_ref[...]).astype(BF16)
    o1, o2, o3, o4 = GLA_KEY, 2 * GLA_KEY, 2 * GLA_KEY + GLA_VAL, 2 * GLA_KEY + 2 * GLA_VAL
    q_ref[...] = (_dot(h, w_ref[:, 0:o1]) * GLA_HK ** -0.5).astype(BF16)
    k_ref[...] = _dot(h, w_ref[:, o1:o2]).astype(BF16)
    v_ref[...] = _dot(h, w_ref[:, o2:o3]).astype(BF16)
    g_ref[...] = _dot(h, w_ref[:, o3:o4]).astype(BF16)
    lr = _dot(h, w_ref[:, o4:]).astype(BF16)
    pre = _dot(lr, wgk_ref[...]) + bgk_ref[...]
    gk_ref[...] = (jnp.minimum(pre, 0.0) - jnp.log1p(jnp.exp(-jnp.abs(pre)))) / GLA_GATE_NORM


def _gla_in(x, nw, w, wgk, bgk, tm):
    t = x.shape[0]

    def tok(width):
        return pl.BlockSpec((tm, width), lambda i: (i, 0))

    def out(width, dtype):
        return jax.ShapeDtypeStruct((t, width), dtype)

    return pl.pallas_call(
        _gla_in_kernel,
        grid=(t // tm,),
        in_specs=[tok(D_MODEL), _const_spec(nw.shape), _const_spec(w.shape), _const_spec(wgk.shape),
                  _const_spec(bgk.shape)],
        out_specs=[tok(GLA_KEY), tok(GLA_KEY), tok(GLA_VAL), tok(GLA_VAL), tok(2 * GLA_KEY)],
        out_shape=[out(GLA_KEY, BF16), out(GLA_KEY, BF16), out(GLA_VAL, BF16), out(GLA_VAL, BF16),
                   out(2 * GLA_KEY, F32)],
        compiler_params=_params(1),
        name="gla_in",
    )(x, nw, w, wgk, bgk)


def _gla_chunk(q, k, v, g, st_ref, reverse):
    L = GLA_CHUNK
    row = lax.broadcasted_iota(jnp.int32, (L, L), 0)
    col = lax.broadcasted_iota(jnp.int32, (L, L), 1)
    tri = (row <= col) if reverse else (row >= col)
    cum = _mask_dot(tri, g)
    tot = cum[0:1] if reverse else cum[L - 1:L]
    qf = q.astype(F32)
    kf = k.astype(F32)
    qt = (qf * jnp.exp(cum)).astype(BF16)
    kt = (kf * jnp.exp(-cum)).astype(BF16)
    ke = (kf * jnp.exp(tot - cum)).astype(BF16)
    st = st_ref[...]
    stb = st.astype(BF16)
    outs = []
    upds = []
    for h in range(GLA_HEADS):
        ks = slice(h * GLA_HK, (h + 1) * GLA_HK)
        vh = v[:, h * GLA_HV:(h + 1) * GLA_HV]
        att = jnp.where(tri, _dot_nt(qt[:, ks], kt[:, ks]), 0.0).astype(BF16)
        outs.append(_dot(att, vh) + _dot_nt(qt[:, ks], stb[:, ks]))
        upds.append(_dot_tn(vh, ke[:, ks]))
    st_ref[...] = st * jnp.exp(tot) + jnp.concatenate(upds, axis=1)
    return jnp.concatenate(outs, axis=1)


def _gla_kernel(qf_ref, qb_ref, kf_ref, kb_ref, vf_ref, vb_ref, gf_ref, gb_ref, of_ref, ob_ref, stf_ref, stb_ref, *, nck):
    @pl.when(pl.program_id(1) == 0)
    def _():
        stf_ref[...] = jnp.zeros_like(stf_ref)
        stb_ref[...] = jnp.zeros_like(stb_ref)

    def body(ci, carry):
        rf = pl.ds(pl.multiple_of(ci * GLA_CHUNK, GLA_CHUNK), GLA_CHUNK)
        rb = pl.ds(pl.multiple_of((nck - 1 - ci) * GLA_CHUNK, GLA_CHUNK), GLA_CHUNK)
        of_ref[0, rf, :] = _gla_chunk(qf_ref[0, rf, :], kf_ref[0, rf, :], vf_ref[0, rf, :], gf_ref[0, rf, :],
                                      stf_ref, False).astype(of_ref.dtype)
        ob_ref[0, rb, :] = _gla_chunk(qb_ref[0, rb, :], kb_ref[0, rb, :], vb_ref[0, rb, :], gb_ref[0, rb, :],
                                      stb_ref, True).astype(ob_ref.dtype)
        return carry

    lax.fori_loop(0, nck, body, 0, unroll=True)


def _gla_scan(q, k, v, gk, tb):
    bsz, seq, _ = q.shape
    nb = seq // tb

    def fwd(width, cb=0):
        return pl.BlockSpec((1, tb, width), lambda b, j: (b, j, cb))

    def bwd(width, cb=0):
        return pl.BlockSpec((1, tb, width), lambda b, j: (b, nb - 1 - j, cb))

    out = jax.ShapeDtypeStruct((bsz, seq, GLA_VAL), BF16)
    return pl.pallas_call(
        functools.partial(_gla_kernel, nck=tb // GLA_CHUNK),
        grid=(bsz, nb),
        in_specs=[fwd(GLA_KEY), bwd(GLA_KEY), fwd(GLA_KEY), bwd(GLA_KEY), fwd(GLA_VAL), bwd(GLA_VAL),
                  fwd(GLA_KEY, 0), bwd(GLA_KEY, 1)],
        out_specs=[fwd(GLA_VAL), bwd(GLA_VAL)],
        out_shape=[out, out],
        scratch_shapes=[pltpu.VMEM((GLA_HV, GLA_KEY), F32), pltpu.VMEM((GLA_HV, GLA_KEY), F32)],
        compiler_params=_params(2),
        name="gla_scan",
    )(q, q, k, k, v, v, gk, gk)


def _gla_out_kernel(of_ref, ob_ref, g_ref, x_ref, gnw_ref, wo_ref, nw_ref, o_ref):
    o = of_ref[...].astype(F32) + ob_ref[...].astype(F32)
    gate = _silu(g_ref[...].astype(F32))
    gnw = gnw_ref[...]
    m = None
    for h in range(GLA_HEADS):
        hs = slice(h * GLA_HV, (h + 1) * GLA_HV)
        oh = (_rms(o[:, hs], gnw) * gate[:, hs]).astype(BF16)
        d = _dot(oh, wo_ref[hs, :])
        m = d if m is None else m + d
    o_ref[...] = x_ref[...] + _rms(m, nw_ref[...])


def _gla_out(of, ob, g, x, gnw, wo, nw, tm):
    t = x.shape[0]
    tok = pl.BlockSpec((tm, D_MODEL), lambda i: (i, 0))
    return pl.pallas_call(
        _gla_out_kernel,
        grid=(t // tm,),
        in_specs=[tok, tok, tok, tok, _const_spec(gnw.shape), _const_spec(wo.shape), _const_spec(nw.shape)],
        out_specs=tok,
        out_shape=jax.ShapeDtypeStruct(x.shape, F32),
        compiler_params=_params(1),
        name="gla_out",
    )(of, ob, g, x, gnw, wo, nw)


def _ffn_kernel(xm_ref, xp_ref, xn_ref, nw_ref, wg_ref, wu_ref, cwg_ref, cwu_ref, wo_ref, nwo_ref, o_ref,
                hcat_ref, *, tm, nt, nchunk):
    j = pl.program_id(1)
    _fill_halo_rows(hcat_ref, xm_ref, xp_ref, xn_ref, nw_ref[...], j, nt, tm)
    hcat = hcat_ref[...]

    def project(c):
        return _dot(hcat, wg_ref[c]), _dot(hcat, wu_ref[c])

    u_next = project(0)
    acc = None
    for c in range(nchunk):
        u_gate, u_up = u_next
        if c + 1 < nchunk:
            u_next = project(c + 1)
        gate = _conv3(u_gate, cwg_ref[c], tm)
        up = _conv3(u_up, cwu_ref[c], tm)
        act = (_gelu_tanh(gate) * up).astype(BF16)
        d = _dot(act, wo_ref[c])
        acc = d if acc is None else acc + d
    o_ref[0] = xm_ref[0] + _rms(acc, nwo_ref[...])


def _ffn(x, nw, wg, wu, cwg, cwu, wo, nwo, tm):
    bsz, seq, _ = x.shape
    nt = seq // tm
    main, prev, nxt = _halo_specs(tm, seq)
    return pl.pallas_call(
        functools.partial(_ffn_kernel, tm=tm, nt=nt, nchunk=wg.shape[0]),
        grid=(bsz, nt),
        in_specs=[main, prev, nxt, _const_spec(nw.shape), _const_spec(wg.shape), _const_spec(wu.shape),
                  _const_spec(cwg.shape), _const_spec(cwu.shape), _const_spec(wo.shape), _const_spec(nwo.shape)],
        out_specs=pl.BlockSpec((1, tm, D_MODEL), lambda b, j: (b, j, 0)),
        out_shape=jax.ShapeDtypeStruct(x.shape, F32),
        scratch_shapes=[pltpu.VMEM((tm + 2 * HALO, D_MODEL), BF16)],
        compiler_params=_params(2),
        name="conv_ffn",
    )(x, x, x, nw, wg, wu, cwg, cwu, wo, nwo)


def _pad_cols(w, width):
    return jnp.pad(w, ((0, 0),) * (w.ndim - 1) + ((0, width - w.shape[-1]),))


def _swap_halves(w):
    half = w.shape[-1] // 2
    return jnp.concatenate([w[..., half:], w[..., :half]], axis=-1)


def _row(v):
    return v.reshape(1, -1).astype(F32)


def _prep_even(w_in, conv_w, conv_b, a_log, dt_bias, d_skip, ssd_norm_w, q_norm_w, w_qb, kv_norm_w, w_kvb, w_out):
    o1 = SSD_INNER
    o2 = o1 + SSD_CONV_DIM
    o3 = o2 + 2 * SSD_HEADS
    o4 = o3 + MLA_Q_LORA
    o5 = o4 + MLA_KV_LORA
    w_kr = w_in[:, o5:]
    wall = jnp.concatenate([w_in[:, :o2], w_in[:, o3:o5], _pad_cols(w_kr, LANES), _pad_cols(_swap_halves(w_kr), LANES),
                            _pad_cols(w_in[:, o2:o3], LANES)], axis=1).astype(BF16)
    cw = jnp.concatenate([conv_w, conv_b[None, :]], axis=0).astype(F32)
    dtb = _pad_cols(_row(dt_bias), LANES)
    alog = _pad_cols(_row(a_log), LANES)
    dskip = _row(jnp.repeat(d_skip, SSD_HEAD_DIM))
    wq3 = w_qb.reshape(MLA_Q_LORA, MLA_HEADS, MLA_NOPE + MLA_ROPE)
    rope = wq3[..., MLA_NOPE:]
    wq = jnp.concatenate([wq3[..., :MLA_NOPE].reshape(MLA_Q_LORA, -1),
                          _pad_cols(rope, LANES).reshape(MLA_Q_LORA, -1),
                          _pad_cols(_swap_halves(rope), LANES).reshape(MLA_Q_LORA, -1)], axis=1).astype(BF16)
    wkv3 = w_kvb.reshape(MLA_KV_LORA, MLA_HEADS, MLA_NOPE + MLA_V)
    wuk = jnp.transpose(wkv3[..., :MLA_NOPE], (1, 2, 0)).astype(BF16)
    wuv = jnp.transpose(wkv3[..., MLA_NOPE:], (1, 0, 2)).astype(BF16)
    return dict(wall=wall, cw=cw, dtb=dtb, alog=alog, dskip=dskip, snw=_row(ssd_norm_w), qnw=_row(q_norm_w), wq=wq,
                wuk=wuk, kvnw=_row(kv_norm_w), wuv=wuv, wo=w_out.astype(BF16))


def _rope_tables(seq):
    inv = 1.0 / (ROPE_THETA ** (jnp.arange(0, MLA_ROPE, 2, dtype=F32) / MLA_ROPE))
    ang = jnp.arange(seq, dtype=F32)[:, None] * inv[None, :]
    cos, sin = jnp.cos(ang), jnp.sin(ang)
    return (_pad_cols(jnp.concatenate([cos, cos], axis=1), LANES),
            _pad_cols(jnp.concatenate([-sin, sin], axis=1), LANES))


def _prep_gla(w_in, w_gk2, b_gk, norm_w, w_out):
    w = _pad_cols(w_in, 2 * GLA_KEY + 2 * GLA_VAL + LANES).astype(BF16)
    wgk = jnp.zeros((LANES, 2 * GLA_KEY), F32)
    for d in range(2):
        wgk = wgk.at[d * GLA_GATE_RANK:(d + 1) * GLA_GATE_RANK, d * GLA_KEY:(d + 1) * GLA_KEY].set(w_gk2[d])
    return dict(w=w, wgk=wgk.astype(BF16), bgk=_row(b_gk), gnw=_row(norm_w), wo=w_out.astype(BF16))


def _prep_ffn(w_in, conv_w, conv_b, w_out):
    nchunk = D_FF // FFN_COL_CHUNK

    def cols(w):
        return jnp.transpose(w.reshape(w.shape[0], nchunk, FFN_COL_CHUNK), (1, 0, 2))

    cw = jnp.concatenate([conv_w, conv_b[None, :], jnp.zeros((4, 2 * D_FF), F32)], axis=0).astype(F32)
    return dict(wg=cols(w_in[:, :D_FF]).astype(BF16), wu=cols(w_in[:, D_FF:]).astype(BF16),
                cwg=cols(cw[:, :D_FF]), cwu=cols(cw[:, D_FF:]),
                wo=w_out.reshape(nchunk, FFN_COL_CHUNK, D_MODEL).astype(BF16))


def _tile(total, pref):
    return min(total, pref)


def _trunk(x, p, tiles=None):
    tiles = dict(dict(even_in=512, ssd=512, mla=256, tok=512, gla=256, ffn=512), **(tiles or {}))
    bsz, seq, d = x.shape
    depth = p['norm_w'].shape[0]
    cos, sin = _rope_tables(seq)
    flat = lambda t: t.reshape(bsz * seq, t.shape[-1])
    for layer in range(depth):
        i = layer // 2
        nw = p['norm_w'][layer].astype(F32)
        if layer % 2 == 0:
            e = _prep_even(p['hyb_w_in'][i], p['ssd_conv_w'][i], p['ssd_conv_b'][i], p['ssd_a_log'][i],
                           p['ssd_dt_bias'][i], p['ssd_d'][i], p['ssd_norm_w'][i], p['mla_q_norm_w'][i],
                           p['mla_w_qb'][i], p['mla_kv_norm_w'][i], p['mla_w_kvb'][i], p['hyb_w_out'][i])
            z, xs, bc, dt, q, k = _even_in(x, nw[0:1], e['wall'], e['cw'], e['dtb'], e['qnw'], e['wq'], e['wuk'],
                                           e['kvnw'], cos, sin, _tile(seq, tiles['even_in']))
            yf, yb = _ssd_scan(xs, bc, dt, e['alog'], _tile(seq, tiles['ssd']))
            om = _mla_attention(q, k, e['wuv'], _tile(seq, tiles['mla']))
            x = _even_out(flat(yf), flat(yb), flat(xs), flat(z), flat(om), flat(x), e['dskip'], e['snw'], e['wo'],
                          nw[1:2], _tile(bsz * seq, tiles['tok'])).reshape(bsz, seq, d)
        else:
            gp = _prep_gla(p['gla_w_in'][i], p['gla_w_gk2'][i], p['gla_b_gk'][i], p['gla_norm_w'][i], p['gla_w_out'][i])
            q, k, v, g, gk = _gla_in(flat(x), nw[0:1], gp['w'], gp['wgk'], gp['bgk'], _tile(bsz * seq, tiles['tok']))
            r3 = lambda t: t.reshape(bsz, seq, t.shape[-1])
            of, ob = _gla_scan(r3(q), r3(k), r3(v), r3(gk), _tile(seq, tiles['gla']))
            x = _gla_out(flat(of), flat(ob), g, flat(x), gp['gnw'], gp['wo'], nw[1:2],
                         _tile(bsz * seq, tiles['tok'])).reshape(bsz, seq, d)
        f = _prep_ffn(p['ffn_w_in'][layer], p['ffn_conv_w'][layer], p['ffn_conv_b'][layer], p['ffn_w_out'][layer])
        x = _ffn(x, nw[2:3], f['wg'], f['wu'], f['cwg'], f['cwu'], f['wo'], nw[3:4], _tile(seq, tiles['ffn']))
    return x


def kernel(x_prompt, x_sample, hyb_w_in, ssd_conv_w, ssd_conv_b, ssd_a_log, ssd_dt_bias, ssd_d, ssd_norm_w, mla_q_norm_w, mla_w_qb, mla_kv_norm_w, mla_w_kvb, hyb_w_out, gla_w_in, gla_w_gk2, gla_b_gk, gla_norm_w, gla_w_out, ffn_w_in, ffn_conv_w, ffn_conv_b, ffn_w_out, norm_w):
    params = dict(hyb_w_in=hyb_w_in, ssd_conv_w=ssd_conv_w, ssd_conv_b=ssd_conv_b, ssd_a_log=ssd_a_log,
                  ssd_dt_bias=ssd_dt_bias, ssd_d=ssd_d, ssd_norm_w=ssd_norm_w,
                  mla_q_norm_w=mla_q_norm_w, mla_w_qb=mla_w_qb, mla_kv_norm_w=mla_kv_norm_w,
                  mla_w_kvb=mla_w_kvb, hyb_w_out=hyb_w_out,
                  gla_w_in=gla_w_in, gla_w_gk2=gla_w_gk2, gla_b_gk=gla_b_gk, gla_norm_w=gla_norm_w,
                  gla_w_out=gla_w_out, ffn_w_in=ffn_w_in, ffn_conv_w=ffn_conv_w, ffn_conv_b=ffn_conv_b,
                  ffn_w_out=ffn_w_out, norm_w=norm_w)
    assert x_prompt.shape[1:] == x_sample.shape[1:]
    nb = x_prompt.shape[0]
    y = _trunk(jnp.concatenate([x_prompt, x_sample], axis=0), params)
    return (y[:nb], y[nb:])
```

```python
import functools
import math

import jax
import jax.numpy as jnp
from jax import lax
from jax.experimental import pallas as pl
from jax.experimental.pallas import tpu as pltpu

F32 = jnp.float32
BF16 = jnp.bfloat16
LOG2_E = math.log2(math.e)

D_MODEL = 1024
EPS = 1e-6

SSD_HEADS = 16
SSD_HEAD_DIM = 64
SSD_INNER = SSD_HEADS * SSD_HEAD_DIM
SSD_GROUPS = 2
SSD_HPG = SSD_HEADS // SSD_GROUPS
SSD_STATE = 128
SSD_CHUNK = 128
SSD_CONV_DIM = SSD_INNER + 2 * SSD_GROUPS * SSD_STATE
SSD_GROUP_WIDTH = SSD_INNER // SSD_GROUPS

MLA_HEADS = 8
MLA_Q_LORA = 256
MLA_KV_LORA = 128
MLA_NOPE = 128
MLA_ROPE = 64
MLA_V = 128
ROPE_THETA = 10000.0
MLA_QK_WIDTH = 256
MLA_GROUP = 4

GLA_HEADS = 4
GLA_KEY = D_MODEL // 2
GLA_VAL = D_MODEL
GLA_HK = GLA_KEY // GLA_HEADS
GLA_HV = GLA_VAL // GLA_HEADS
GLA_GATE_RANK = 16
GLA_GATE_NORM = 16.0
GLA_CHUNK = 64

D_FF = 2816
FFN_COL_CHUNK = 256

LANES = 128
SUBLANES = 8
HALO = 2 * SUBLANES
VMEM_LIMIT_BYTES = 56 * 1024 * 1024

_E_Z = 0
_E_XBC = _E_Z + SSD_INNER
_E_QA = _E_XBC + SSD_CONV_DIM
_E_CKV = _E_QA + MLA_Q_LORA
_E_KR = _E_CKV + MLA_KV_LORA
_E_KRSW = _E_KR + LANES
_E_DT = _E_KRSW + LANES
_E_END = _E_DT + LANES


def _dot(a, b):
    return jnp.dot(a, b, preferred_element_type=F32)


def _dot_nt(a, b):
    return lax.dot_general(a, b, (((1,), (1,)), ((), ())), preferred_element_type=F32)


def _dot_tn(a, b):
    return lax.dot_general(a, b, (((0,), (0,)), ((), ())), preferred_element_type=F32)


def _mask_dot(mask, v):
    t = mask.astype(BF16)
    h1 = v.astype(BF16)
    r1 = v - h1.astype(F32)
    h2 = r1.astype(BF16)
    h3 = (r1 - h2.astype(F32)).astype(BF16)
    return _dot(t, h1) + _dot(t, h2) + _dot(t, h3)


def _rms(x, w):
    return x * lax.rsqrt(jnp.mean(x * x, axis=-1, keepdims=True) + EPS) * w


def _softplus(x):
    return jnp.maximum(x, 0.0) + jnp.log1p(jnp.exp(-jnp.abs(x)))


def _silu(x):
    return x * jax.nn.sigmoid(x)


def _gelu_tanh(x):
    k = -2.0 * math.sqrt(2.0 / math.pi) * LOG2_E
    return x / (1.0 + jnp.exp2((x * x * (k * 0.044715) + k) * x))


def _params(n_axes, flags=None):
    return pltpu.CompilerParams(dimension_semantics=("arbitrary",) * n_axes,
                                vmem_limit_bytes=VMEM_LIMIT_BYTES, flags=flags)


def _interleave(streams):
    streams = list(streams)
    while streams:
        alive = []
        for stream in streams:
            try:
                next(stream)
                alive.append(stream)
            except StopIteration:
                pass
        streams = alive


def _const_spec(shape):
    nd = len(shape)
    return pl.BlockSpec(shape, lambda *_: (0,) * nd, pipeline_mode=pl.Buffered(1))


def _halo_specs(tm, seq):
    per = tm // SUBLANES
    last = seq // SUBLANES - 1
    main = pl.BlockSpec((1, tm, D_MODEL), lambda b, j: (b, j, 0))
    prev = pl.BlockSpec((1, SUBLANES, D_MODEL), lambda b, j: (b, jnp.maximum(j * per - 1, 0), 0))
    nxt = pl.BlockSpec((1, SUBLANES, D_MODEL), lambda b, j: (b, jnp.minimum((j + 1) * per, last), 0))
    return main, prev, nxt


def _fill_halo_rows(hcat_ref, xm_ref, xp_ref, xn_ref, nw, j, nt, tm):
    hn = jnp.where(j == nt - 1, 0.0, _rms(xn_ref[0], nw))
    hp = jnp.where(j == 0, 0.0, _rms(xp_ref[0], nw))
    hcat_ref[0:HALO, :] = jnp.concatenate([hn, hp], axis=0).astype(BF16)
    hcat_ref[HALO:, :] = _rms(xm_ref[0], nw).astype(BF16)


def _conv3(u, cw, tm):
    rows = tm + HALO
    um1 = pltpu.roll(u, 1, 0)[HALO:]
    up1 = pltpu.roll(u, rows - 1, 0)[HALO:]
    return um1 * cw[0:1] + u[HALO:] * cw[1:2] + up1 * cw[2:3] + cw[3:4]


def _even_in_kernel(xm_ref, xp_ref, xn_ref, nw_ref, wall_ref, cw_ref, dtb_ref, qnw_ref, wq_ref, wuk_ref,
                    kvnw_ref, cos_ref, sin_ref,
                    z_ref, xs_ref, bc_ref, dt_ref, q_ref, k_ref, hcat_ref, *, tm, nt):
    j = pl.program_id(1)
    _fill_halo_rows(hcat_ref, xm_ref, xp_ref, xn_ref, nw_ref[...], j, nt, tm)

    hcat = hcat_ref[...]
    for c0 in range(0, SSD_CONV_DIM, 512):
        u = _dot(hcat, wall_ref[:, _E_XBC + c0:_E_XBC + c0 + 512])
        y = _silu(_conv3(u, cw_ref[:, c0:c0 + 512], tm)).astype(BF16)
        if c0 < SSD_INNER:
            xs_ref[0, :, c0:c0 + 512] = y
        else:
            bc_ref[0] = y

    hm = hcat_ref[HALO:, :]
    z_ref[0] = _dot(hm, wall_ref[:, _E_Z:_E_XBC]).astype(BF16)
    rest = _dot(hm, wall_ref[:, _E_QA:_E_END])
    o = -_E_QA
    qa = rest[:, o + _E_QA:o + _E_CKV]
    ckv = rest[:, o + _E_CKV:o + _E_KR]
    kr = rest[:, o + _E_KR:o + _E_KRSW]
    krsw = rest[:, o + _E_KRSW:o + _E_DT]
    dtr = rest[:, o + _E_DT:o + _E_END]
    dt_ref[0] = _softplus(dtr + dtb_ref[...])

    cos = cos_ref[...]
    sin = sin_ref[...]
    scale = (MLA_NOPE + MLA_ROPE) ** -0.5 * LOG2_E
    hq = _rms(qa, qnw_ref[...]).astype(BF16)
    nh = MLA_HEADS * LANES
    qall = _dot(hq, wq_ref[...])
    for h in range(MLA_HEADS):
        qn = qall[:, h * LANES:(h + 1) * LANES].astype(BF16)
        ql = _dot(qn, wuk_ref[h])
        qr = qall[:, nh + h * LANES:nh + (h + 1) * LANES] * cos + qall[:, 2 * nh + h * LANES:2 * nh + (h + 1) * LANES] * sin
        q_ref[0, h, :, 0:LANES] = (ql * scale).astype(BF16)
        q_ref[0, h, :, LANES:MLA_QK_WIDTH] = (qr * scale).astype(BF16)
    k_ref[0, :, 0:LANES] = _rms(ckv, kvnw_ref[...]).astype(BF16)
    k_ref[0, :, LANES:2 * LANES] = (kr * cos + krsw * sin).astype(BF16)


def _even_in(x, nw, wall, cw, dtb, qnw, wq, wuk, kvnw, cos, sin, tm):
    bsz, seq, _ = x.shape
    nt = seq // tm
    main, prev, nxt = _halo_specs(tm, seq)

    def tok(width):
        return pl.BlockSpec((1, tm, width), lambda b, j: (b, j, 0))

    def out(width, dtype):
        return jax.ShapeDtypeStruct((bsz, seq, width), dtype)

    rope_spec = pl.BlockSpec((tm, LANES), lambda b, j: (j, 0))
    return pl.pallas_call(
        functools.partial(_even_in_kernel, tm=tm, nt=nt),
        grid=(bsz, nt),
        in_specs=[main, prev, nxt, _const_spec(nw.shape), _const_spec(wall.shape), _const_spec(cw.shape),
                  _const_spec(dtb.shape), _const_spec(qnw.shape), _const_spec(wq.shape), _const_spec(wuk.shape),
                  _const_spec(kvnw.shape), rope_spec, rope_spec],
        out_specs=[tok(SSD_INNER), tok(SSD_INNER), tok(2 * SSD_GROUPS * SSD_STATE), tok(LANES),
                   pl.BlockSpec((1, MLA_HEADS, tm, MLA_QK_WIDTH), lambda b, j: (b, 0, j, 0)), tok(MLA_QK_WIDTH)],
        out_shape=[out(SSD_INNER, BF16), out(SSD_INNER, BF16), out(2 * SSD_GROUPS * SSD_STATE, BF16),
                   out(LANES, F32), jax.ShapeDtypeStruct((bsz, MLA_HEADS, seq, MLA_QK_WIDTH), BF16),
                   out(MLA_QK_WIDTH, BF16)],
        scratch_shapes=[pltpu.VMEM((tm + HALO, D_MODEL), BF16)],
        compiler_params=_params(2),
        name="even_in",
    )(x, x, x, nw, wall, cw, dtb, qnw, wq, wuk, kvnw, cos, sin)


def _expand_heads(v, expand):
    hi = v.astype(BF16)
    lo = (v - hi.astype(F32)).astype(BF16)
    return _dot(hi, expand) + _dot(lo, expand)


def _ssd_stream(x_ref, bc_ref, dt_ref, y_ref, st_ref, a, reverse, lane0, nck):
    L = SSD_CHUNK
    row = lax.broadcasted_iota(jnp.int32, (L, L), 0)
    col = lax.broadcasted_iota(jnp.int32, (L, L), 1)
    tri = (row <= col) if reverse else (row >= col)
    er = lax.broadcasted_iota(jnp.int32, (LANES, SSD_INNER), 0)
    ec = lax.broadcasted_iota(jnp.int32, (LANES, SSD_INNER), 1)
    head_of_lane = lax.shift_right_logical(ec, int(math.log2(SSD_HEAD_DIM)))
    expand = jnp.where(er - lane0 == head_of_lane, 1.0, 0.0).astype(BF16)
    lane = lax.broadcasted_iota(jnp.int32, (L, LANES), 1)
    low_half = lane < SSD_HEAD_DIM
    gw = SSD_GROUP_WIDTH
    groups = [slice(g * gw, (g + 1) * gw) for g in range(SSD_GROUPS)]
    staged = []
    for ci in (range(nck - 1, -1, -1) if reverse else range(nck)):
        rows = pl.ds(ci * L, L)
        dt = dt_ref[rows, :]
        cum = _mask_dot(tri, dt * a)
        yield
        cum_t = cum.T
        dt_t = dt.T
        tot = cum[0:1] if reverse else cum[L - 1:L]
        ecum = _expand_heads(jnp.exp(cum), expand)
        wend = _expand_heads(jnp.exp(tot - cum) * dt, expand)
        etot = _expand_heads(jnp.broadcast_to(jnp.exp(tot), (SUBLANES, LANES)), expand)[0:1]
        yield
        x = x_ref[rows, :]
        bc = bc_ref[rows, :]
        xw = (x.astype(F32) * wend).astype(BF16)
        bs = [bc[:, g * SSD_STATE:(g + 1) * SSD_STATE] for g in range(SSD_GROUPS)]
        cs = [bc[:, (SSD_GROUPS + g) * SSD_STATE:(SSD_GROUPS + g + 1) * SSD_STATE] for g in range(SSD_GROUPS)]
        cb = [_dot_nt(cs[g], bs[g]) for g in range(SSD_GROUPS)]
        upd = jnp.concatenate([_dot_tn(bs[g], xw[:, groups[g]]) for g in range(SSD_GROUPS)], axis=1)
        yield
        intra = []
        for g in range(SSD_GROUPS):
            for p in range(SSD_HPG // 2):
                lo = g * gw + p * LANES
                x_pair = x[:, lo:lo + LANES]
                zero = jnp.zeros_like(x_pair)
                x_stack = jnp.concatenate([jnp.where(low_half, x_pair, zero), jnp.where(low_half, zero, x_pair)],
                                          axis=0)
                pair = []
                for e in (2 * p, 2 * p + 1):
                    c = lane0 + g * SSD_HPG + e
                    diff = cum[:, c:c + 1] - cum_t[c:c + 1, :]
                    decay = jnp.exp(jnp.where(tri, diff, -jnp.inf))
                    pair.append((cb[g] * decay * dt_t[c:c + 1, :]).astype(BF16))
                intra.append(_dot(jnp.concatenate(pair, axis=1), x_stack))
            yield
        staged.append((rows, cs, ecum, etot, intra, upd))
    for rows, cs, ecum, etot, intra, upd in staged:
        st = st_ref[...]
        stb = st.astype(BF16)
        ys = []
        for g in range(SSD_GROUPS):
            y_inter = _dot(cs[g], stb[:, groups[g]])
            for p in range(SSD_HPG // 2):
                lo = g * gw + p * LANES
                ys.append(intra[g * (SSD_HPG // 2) + p] + y_inter[:, p * LANES:(p + 1) * LANES] * ecum[:, lo:lo + LANES])
        st_ref[...] = st * etot + upd
        yield
        y_ref[rows, :] = jnp.concatenate(ys, axis=1).astype(y_ref.dtype)
        yield


def _ssd_kernel(xf_ref, xb_ref, bcf_ref, bcb_ref, dtf_ref, dtb_ref, alog_ref, yf_ref, yb_ref, stf_ref, stb_ref,
                *, nck, nseq):
    @pl.when(pl.program_id(1) == 0)
    def _():
        stf_ref[...] = jnp.zeros_like(stf_ref)
        stb_ref[...] = jnp.zeros_like(stb_ref)

    a = -jnp.exp(alog_ref[...])
    streams = []
    for s in range(nseq):
        streams.append(_ssd_stream(xf_ref.at[s], bcf_ref.at[s], dtf_ref.at[s], yf_ref.at[s], stf_ref.at[s], a,
                                   False, 0, nck))
        streams.append(_ssd_stream(xb_ref.at[s], bcb_ref.at[s], dtb_ref.at[s], yb_ref.at[s], stb_ref.at[s], a,
                                   True, SSD_HEADS, nck))
    _interleave(streams)


def _ssd_scan(xs, bc, dt, alog, tb, nseq):
    bsz, seq, _ = xs.shape
    nb = seq // tb

    def fwd(width):
        return pl.BlockSpec((nseq, tb, width), lambda b, j: (b, j, 0))

    def bwd(width):
        return pl.BlockSpec((nseq, tb, width), lambda b, j: (b, nb - 1 - j, 0))

    out = jax.ShapeDtypeStruct((bsz, seq, SSD_INNER), BF16)
    state = pltpu.VMEM((nseq, SSD_STATE, SSD_INNER), F32)
    return pl.pallas_call(
        functools.partial(_ssd_kernel, nck=tb // SSD_CHUNK, nseq=nseq),
        grid=(bsz // nseq, nb),
        in_specs=[fwd(SSD_INNER), bwd(SSD_INNER), fwd(bc.shape[-1]), bwd(bc.shape[-1]), fwd(LANES), bwd(LANES),
                  _const_spec(alog.shape)],
        out_specs=[fwd(SSD_INNER), bwd(SSD_INNER)],
        out_shape=[out, out],
        scratch_shapes=[state, state],
        compiler_params=_params(2),
        name="ssd_scan",
    )(xs, xs, bc, bc, dt, dt, alog)


def _mla_kernel(q_ref, k_ref, wuv_ref, o_ref, *, tq):
    k = k_ref[0]
    v_t = jnp.concatenate([k[:, 0:MLA_KV_LORA].astype(F32).T.astype(BF16), jnp.ones((HALO, k.shape[0]), BF16)], axis=0)
    ngroups = MLA_HEADS // MLA_GROUP

    def scores(g):
        q = q_ref[0, g * MLA_GROUP:(g + 1) * MLA_GROUP].reshape(MLA_GROUP * tq, MLA_QK_WIDTH)
        return _dot_nt(q, k)

    s_next = scores(0)
    for g in range(ngroups):
        s = s_next
        if g + 1 < ngroups:
            s_next = scores(g + 1)
        m = jnp.max(s, axis=-1, keepdims=True)
        p = jnp.exp2(s - m).astype(BF16)
        o_t = _dot_nt(v_t, p)
        o_lat_t = o_t[0:MLA_KV_LORA] / o_t[MLA_KV_LORA:MLA_KV_LORA + 1]
        for i in range(MLA_GROUP):
            h = g * MLA_GROUP + i
            o_lat = o_lat_t[:, i * tq:(i + 1) * tq].T.astype(BF16)
            o_ref[0, :, h * MLA_V:(h + 1) * MLA_V] = _dot(o_lat, wuv_ref[h]).astype(o_ref.dtype)


def _mla_attention(q, k, wuv, tq):
    bsz, _, seq, _ = q.shape
    return pl.pallas_call(
        functools.partial(_mla_kernel, tq=tq),
        grid=(bsz, seq // tq),
        in_specs=[pl.BlockSpec((1, MLA_HEADS, tq, MLA_QK_WIDTH), lambda b, j: (b, 0, j, 0)),
                  pl.BlockSpec((1, seq, k.shape[-1]), lambda b, j: (b, 0, 0)),
                  _const_spec(wuv.shape)],
        out_specs=pl.BlockSpec((1, tq, MLA_HEADS * MLA_V), lambda b, j: (b, j, 0)),
        out_shape=jax.ShapeDtypeStruct((bsz, seq, MLA_HEADS * MLA_V), BF16),
        compiler_params=_params(2),
        name="mla_attention",
    )(q, k, wuv)


def _even_out_kernel(yf_ref, yb_ref, xs_ref, z_ref, om_ref, x_ref, dskip_ref, snw_ref, wo_ref, nw_ref, o_ref):
    y = yf_ref[...].astype(F32) + yb_ref[...].astype(F32) + xs_ref[...].astype(F32) * dskip_ref[...]
    y = y * _silu(z_ref[...].astype(F32))
    snw = snw_ref[...]
    gw = SSD_GROUP_WIDTH
    m = _dot(om_ref[...], wo_ref[SSD_INNER:, :])
    for g in range(SSD_GROUPS):
        yg = _rms(y[:, g * gw:(g + 1) * gw], snw[:, g * gw:(g + 1) * gw]).astype(BF16)
        m = m + _dot(yg, wo_ref[g * gw:(g + 1) * gw, :])
    o_ref[...] = x_ref[...] + _rms(m, nw_ref[...])


def _even_out(yf, yb, xs, z, om, x, dskip, snw, wo, nw, tm):
    t = x.shape[0]
    tok = pl.BlockSpec((tm, D_MODEL), lambda i: (i, 0))
    return pl.pallas_call(
        _even_out_kernel,
        grid=(t // tm,),
        in_specs=[tok, tok, tok, tok, tok, tok, _const_spec(dskip.shape), _const_spec(snw.shape),
                  _const_spec(wo.shape), _const_spec(nw.shape)],
        out_specs=tok,
        out_shape=jax.ShapeDtypeStruct(x.shape, F32),
        compiler_params=_params(1),
        name="even_out",
    )(yf, yb, xs, z, om, x, dskip, snw, wo, nw)


def _gla_in_kernel(x_ref, nw_ref, w_ref, wgk_ref, bgk_ref, q_ref, k_ref, v_ref, g_ref, gk_ref):
    h = _rms(x_ref[...], nw_ref[...]).astype(BF16)
    o1, o2, o3, o4 = GLA_KEY, 2 * GLA_KEY, 2 * GLA_KEY + GLA_VAL, 2 * GLA_KEY + 2 * GLA_VAL
    q_ref[...] = (_dot(h, w_ref[:, 0:o1]) * GLA_HK ** -0.5).astype(BF16)
    k_ref[...] = _dot(h, w_ref[:, o1:o2]).astype(BF16)
    v_ref[...] = _dot(h, w_ref[:, o2:o3]).astype(BF16)
    g_ref[...] = _dot(h, w_ref[:, o3:o4]).astype(BF16)
    lr = _dot(h, w_ref[:, o4:]).astype(BF16)
    pre = _dot(lr, wgk_ref[...]) + bgk_ref[...]
    gk_ref[...] = (jnp.minimum(pre, 0.0) - jnp.log1p(jnp.exp(-jnp.abs(pre)))) / GLA_GATE_NORM


def _gla_in(x, nw, w, wgk, bgk, tm):
    t = x.shape[0]

    def tok(width):
        return pl.BlockSpec((tm, width), lambda i: (i, 0))

    def out(width, dtype):
        return jax.ShapeDtypeStruct((t, width), dtype)

    return pl.pallas_call(
        _gla_in_kernel,
        grid=(t // tm,),
        in_specs=[tok(D_MODEL), _const_spec(nw.shape), _const_spec(w.shape), _const_spec(wgk.shape),
                  _const_spec(bgk.shape)],
        out_specs=[tok(GLA_KEY), tok(GLA_KEY), tok(GLA_VAL), tok(GLA_VAL), tok(2 * GLA_KEY)],
        out_shape=[out(GLA_KEY, BF16), out(GLA_KEY, BF16), out(GLA_VAL, BF16), out(GLA_VAL, BF16),
                   out(2 * GLA_KEY, F32)],
        compiler_params=_params(1),
        name="gla_in",
    )(x, nw, w, wgk, bgk)


def _gla_stream(q_ref, k_ref, v_ref, g_ref, o_ref, st_ref, reverse, nck):
    L = GLA_CHUNK
    row = lax.broadcasted_iota(jnp.int32, (L, L), 0)
    col = lax.broadcasted_iota(jnp.int32, (L, L), 1)
    tri = (row <= col) if reverse else (row >= col)
    keys = [slice(h * GLA_HK, (h + 1) * GLA_HK) for h in range(GLA_HEADS)]
    vals = [slice(h * GLA_HV, (h + 1) * GLA_HV) for h in range(GLA_HEADS)]
    staged = []
    for ci in (range(nck - 1, -1, -1) if reverse else range(nck)):
        rows = pl.ds(ci * L, L)
        cum = _mask_dot(tri, g_ref[rows, :])
        yield
        tot = cum[0:1] if reverse else cum[L - 1:L]
        qf = q_ref[rows, :].astype(F32)
        kf = k_ref[rows, :].astype(F32)
        qt = (qf * jnp.exp(cum)).astype(BF16)
        kt = (kf * jnp.exp(-cum)).astype(BF16)
        ke = (kf * jnp.exp(tot - cum)).astype(BF16)
        yield
        att = [_dot_nt(qt[:, ks], kt[:, ks]) for ks in keys]
        yield
        v = v_ref[rows, :]
        intra = [_dot(jnp.where(tri, att[h], 0.0).astype(BF16), v[:, vals[h]]) for h in range(GLA_HEADS)]
        upd = [_dot_tn(v[:, vals[h]], ke[:, keys[h]]) for h in range(GLA_HEADS)]
        yield
        staged.append((rows, qt, jnp.exp(tot), intra, jnp.concatenate(upd, axis=1)))
    for rows, qt, etot, intra, upd in staged:
        st = st_ref[...]
        stb = st.astype(BF16)
        outs = [intra[h] + _dot_nt(qt[:, keys[h]], stb[:, keys[h]]) for h in range(GLA_HEADS)]
        st_ref[...] = st * etot + upd
        yield
        o_ref[rows, :] = jnp.concatenate(outs, axis=1).astype(o_ref.dtype)
        yield


def _gla_kernel(qf_ref, qb_ref, kf_ref, kb_ref, vf_ref, vb_ref, gf_ref, gb_ref, of_ref, ob_ref, stf_ref, stb_ref,
                *, nck, nseq):
    @pl.when(pl.program_id(1) == 0)
    def _():
        stf_ref[...] = jnp.zeros_like(stf_ref)
        stb_ref[...] = jnp.zeros_like(stb_ref)

    streams = []
    for s in range(nseq):
        streams.append(_gla_stream(qf_ref.at[s], kf_ref.at[s], vf_ref.at[s], gf_ref.at[s], of_ref.at[s],
                                   stf_ref.at[s], False, nck))
        streams.append(_gla_stream(qb_ref.at[s], kb_ref.at[s], vb_ref.at[s], gb_ref.at[s], ob_ref.at[s],
                                   stb_ref.at[s], True, nck))
    _interleave(streams)


def _gla_scan(q, k, v, gk, tb, nseq):
    bsz, seq, _ = q.shape
    nb = seq // tb

    def fwd(width, cb=0):
        return pl.BlockSpec((nseq, tb, width), lambda b, j: (b, j, cb))

    def bwd(width, cb=0):
        return pl.BlockSpec((nseq, tb, width), lambda b, j: (b, nb - 1 - j, cb))

    out = jax.ShapeDtypeStruct((bsz, seq, GLA_VAL), BF16)
    state = pltpu.VMEM((nseq, GLA_HV, GLA_KEY), F32)
    return pl.pallas_call(
        functools.partial(_gla_kernel, nck=tb // GLA_CHUNK, nseq=nseq),
        grid=(bsz // nseq, nb),
        in_specs=[fwd(GLA_KEY), bwd(GLA_KEY), fwd(GLA_KEY), bwd(GLA_KEY), fwd(GLA_VAL), bwd(GLA_VAL),
                  fwd(GLA_KEY, 0), bwd(GLA_KEY, 1)],
        out_specs=[fwd(GLA_VAL), bwd(GLA_VAL)],
        out_shape=[out, out],
        scratch_shapes=[state, state],
        compiler_params=_params(2),
        name="gla_scan",
    )(q, q, k, k, v, v, gk, gk)


def _gla_out_kernel(of_ref, ob_ref, g_ref, x_ref, gnw_ref, wo_ref, nw_ref, o_ref):
    o = of_ref[...].astype(F32) + ob_ref[...].astype(F32)
    gate = _silu(g_ref[...].astype(F32))
    gnw = gnw_ref[...]
    m = None
    for h in range(GLA_HEADS):
        hs = slice(h * GLA_HV, (h + 1) * GLA_HV)
        oh = (_rms(o[:, hs], gnw) * gate[:, hs]).astype(BF16)
        d = _dot(oh, wo_ref[hs, :])
        m = d if m is None else m + d
    o_ref[...] = x_ref[...] + _rms(m, nw_ref[...])


def _gla_out(of, ob, g, x, gnw, wo, nw, tm):
    t = x.shape[0]
    tok = pl.BlockSpec((tm, D_MODEL), lambda i: (i, 0))
    return pl.pallas_call(
        _gla_out_kernel,
        grid=(t // tm,),
        in_specs=[tok, tok, tok, tok, _const_spec(gnw.shape), _const_spec(wo.shape), _const_spec(nw.shape)],
        out_specs=tok,
        out_shape=jax.ShapeDtypeStruct(x.shape, F32),
        compiler_params=_params(1),
        name="gla_out",
    )(of, ob, g, x, gnw, wo, nw)


def _ffn_kernel(xm_ref, xp_ref, xn_ref, nw_ref, wg_ref, wu_ref, cwg_ref, cwu_ref, wo_ref, nwo_ref, o_ref,
                hcat_ref, *, tm, nt, nchunk):
    j = pl.program_id(1)
    _fill_halo_rows(hcat_ref, xm_ref, xp_ref, xn_ref, nw_ref[...], j, nt, tm)
    hcat = hcat_ref[...]

    def project(c):
        return _dot(hcat, wg_ref[c]), _dot(hcat, wu_ref[c])

    u_next = project(0)
    acc = None
    acts = []
    for c in range(nchunk):
        u_gate, u_up = u_next
        if c + 1 < nchunk:
            u_next = project(c + 1)
        gate = _conv3(u_gate, cwg_ref[c], tm)
        up = _conv3(u_up, cwu_ref[c], tm)
        acts.append((_gelu_tanh(gate) * up).astype(BF16))
        if len(acts) == 2 or c + 1 == nchunk:
            c0 = c + 1 - len(acts)
            act = acts[0] if len(acts) == 1 else jnp.concatenate(acts, axis=1)
            d = _dot(act, wo_ref[c0 * FFN_COL_CHUNK:(c + 1) * FFN_COL_CHUNK, :])
            acc = d if acc is None else acc + d
            acts = []
    o_ref[0] = xm_ref[0] + _rms(acc, nwo_ref[...])


def _ffn(x, nw, wg, wu, cwg, cwu, wo, nwo, tm):
    bsz, seq, _ = x.shape
    nt = seq // tm
    main, prev, nxt = _halo_specs(tm, seq)
    return pl.pallas_call(
        functools.partial(_ffn_kernel, tm=tm, nt=nt, nchunk=wg.shape[0]),
        grid=(bsz, nt),
        in_specs=[main, prev, nxt, _const_spec(nw.shape), _const_spec(wg.shape), _const_spec(wu.shape),
                  _const_spec(cwg.shape), _const_spec(cwu.shape), _const_spec(wo.shape), _const_spec(nwo.shape)],
        out_specs=pl.BlockSpec((1, tm, D_MODEL), lambda b, j: (b, j, 0)),
        out_shape=jax.ShapeDtypeStruct(x.shape, F32),
        scratch_shapes=[pltpu.VMEM((tm + HALO, D_MODEL), BF16)],
        compiler_params=_params(2),
        name="conv_ffn",
    )(x, x, x, nw, wg, wu, cwg, cwu, wo, nwo)


def _pad_cols(w, width):
    return jnp.pad(w, ((0, 0),) * (w.ndim - 1) + ((0, width - w.shape[-1]),))


def _swap_halves(w):
    half = w.shape[-1] // 2
    return jnp.concatenate([w[..., half:], w[..., :half]], axis=-1)


def _row(v):
    return v.reshape(1, -1).astype(F32)


def _prep_even(w_in, conv_w, conv_b, a_log, dt_bias, d_skip, ssd_norm_w, q_norm_w, w_qb, kv_norm_w, w_kvb, w_out):
    o1 = SSD_INNER
    o2 = o1 + SSD_CONV_DIM
    o3 = o2 + 2 * SSD_HEADS
    o4 = o3 + MLA_Q_LORA
    o5 = o4 + MLA_KV_LORA
    w_kr = w_in[:, o5:]
    wall = jnp.concatenate([w_in[:, :o2], w_in[:, o3:o5], _pad_cols(w_kr, LANES), _pad_cols(_swap_halves(w_kr), LANES),
                            _pad_cols(w_in[:, o2:o3], LANES)], axis=1).astype(BF16)
    cw = jnp.concatenate([conv_w, conv_b[None, :]], axis=0).astype(F32)
    dtb = _pad_cols(_row(dt_bias), LANES)
    alog = _pad_cols(_row(a_log), LANES)
    dskip = _row(jnp.repeat(d_skip, SSD_HEAD_DIM))
    wq3 = w_qb.reshape(MLA_Q_LORA, MLA_HEADS, MLA_NOPE + MLA_ROPE)
    rope = wq3[..., MLA_NOPE:]
    wq = jnp.concatenate([wq3[..., :MLA_NOPE].reshape(MLA_Q_LORA, -1),
                          _pad_cols(rope, LANES).reshape(MLA_Q_LORA, -1),
                          _pad_cols(_swap_halves(rope), LANES).reshape(MLA_Q_LORA, -1)], axis=1).astype(BF16)
    wkv3 = w_kvb.reshape(MLA_KV_LORA, MLA_HEADS, MLA_NOPE + MLA_V)
    wuk = jnp.transpose(wkv3[..., :MLA_NOPE], (1, 2, 0)).astype(BF16)
    wuv = jnp.transpose(wkv3[..., MLA_NOPE:], (1, 0, 2)).astype(BF16)
    return dict(wall=wall, cw=cw, dtb=dtb, alog=alog, dskip=dskip, snw=_row(ssd_norm_w), qnw=_row(q_norm_w), wq=wq,
                wuk=wuk, kvnw=_row(kv_norm_w), wuv=wuv, wo=w_out.astype(BF16))


def _rope_tables(seq):
    inv = 1.0 / (ROPE_THETA ** (jnp.arange(0, MLA_ROPE, 2, dtype=F32) / MLA_ROPE))
    ang = jnp.arange(seq, dtype=F32)[:, None] * inv[None, :]
    cos, sin = jnp.cos(ang), jnp.sin(ang)
    return (_pad_cols(jnp.concatenate([cos, cos], axis=1), LANES),
            _pad_cols(jnp.concatenate([-sin, sin], axis=1), LANES))


def _prep_gla(w_in, w_gk2, b_gk, norm_w, w_out):
    w = _pad_cols(w_in, 2 * GLA_KEY + 2 * GLA_VAL + LANES).astype(BF16)
    wgk = jnp.zeros((LANES, 2 * GLA_KEY), F32)
    for d in range(2):
        wgk = wgk.at[d * GLA_GATE_RANK:(d + 1) * GLA_GATE_RANK, d * GLA_KEY:(d + 1) * GLA_KEY].set(w_gk2[d])
    return dict(w=w, wgk=wgk.astype(BF16), bgk=_row(b_gk), gnw=_row(norm_w), wo=w_out.astype(BF16))


def _prep_ffn(w_in, conv_w, conv_b, w_out):
    nchunk = D_FF // FFN_COL_CHUNK

    def cols(w):
        return jnp.transpose(w.reshape(w.shape[0], nchunk, FFN_COL_CHUNK), (1, 0, 2))

    cw = jnp.concatenate([conv_w, conv_b[None, :], jnp.zeros((4, 2 * D_FF), F32)], axis=0).astype(F32)
    return dict(wg=cols(w_in[:, :D_FF]).astype(BF16), wu=cols(w_in[:, D_FF:]).astype(BF16),
                cwg=cols(cw[:, :D_FF]), cwu=cols(cw[:, D_FF:]),
                wo=w_out.astype(BF16))


def _tile(total, pref):
    return min(total, pref)


def _trunk(x, p, tiles=None):
    tiles = dict(dict(even_in=512, ssd=512, ssd_seqs=1, mla=256, tok=512, gla=256, gla_seqs=2, ffn=512), **(tiles or {}))
    bsz, seq, d = x.shape
    depth = p['norm_w'].shape[0]
    cos, sin = _rope_tables(seq)
    flat = lambda t: t.reshape(bsz * seq, t.shape[-1])
    for layer in range(depth):
        i = layer // 2
        nw = p['norm_w'][layer].astype(F32)
        if layer % 2 == 0:
            e = _prep_even(p['hyb_w_in'][i], p['ssd_conv_w'][i], p['ssd_conv_b'][i], p['ssd_a_log'][i],
                           p['ssd_dt_bias'][i], p['ssd_d'][i], p['ssd_norm_w'][i], p['mla_q_norm_w'][i],
                           p['mla_w_qb'][i], p['mla_kv_norm_w'][i], p['mla_w_kvb'][i], p['hyb_w_out'][i])
            z, xs, bc, dt, q, k = _even_in(x, nw[0:1], e['wall'], e['cw'], e['dtb'], e['qnw'], e['wq'], e['wuk'],
                                           e['kvnw'], cos, sin, _tile(seq, tiles['even_in']))
            yf, yb = _ssd_scan(xs, bc, dt, e['alog'], _tile(seq, tiles['ssd']), math.gcd(bsz, tiles['ssd_seqs']))
            om = _mla_attention(q, k, e['wuv'], _tile(seq, tiles['mla']))
            x = _even_out(flat(yf), flat(yb), flat(xs), flat(z), flat(om), flat(x), e['dskip'], e['snw'], e['wo'],
                          nw[1:2], _tile(bsz * seq, tiles['tok'])).reshape(bsz, seq, d)
        else:
            gp = _prep_gla(p['gla_w_in'][i], p['gla_w_gk2'][i], p['gla_b_gk'][i], p['gla_norm_w'][i], p['gla_w_out'][i])
            q, k, v, g, gk = _gla_in(flat(x), nw[0:1], gp['w'], gp['wgk'], gp['bgk'], _tile(bsz * seq, tiles['tok']))
            r3 = lambda t: t.reshape(bsz, seq, t.shape[-1])
            of, ob = _gla_scan(r3(q), r3(k), r3(v), r3(gk), _tile(seq, tiles['gla']), math.gcd(bsz, tiles['gla_seqs']))
            x = _gla_out(flat(of), flat(ob), g, flat(x), gp['gnw'], gp['wo'], nw[1:2],
                         _tile(bsz * seq, tiles['tok'])).reshape(bsz, seq, d)
        f = _prep_ffn(p['ffn_w_in'][layer], p['ffn_conv_w'][layer], p['ffn_conv_b'][layer], p['ffn_w_out'][layer])
        x = _ffn(x, nw[2:3], f['wg'], f['wu'], f['cwg'], f['cwu'], f['wo'], nw[3:4], _tile(seq, tiles['ffn']))
    return x


def kernel(x_prompt, x_sample, hyb_w_in, ssd_conv_w, ssd_conv_b, ssd_a_log, ssd_dt_bias, ssd_d, ssd_norm_w, mla_q_norm_w, mla_w_qb, mla_kv_norm_w, mla_w_kvb, hyb_w_out, gla_w_in, gla_w_gk2, gla_b_gk, gla_norm_w, gla_w_out, ffn_w_in, ffn_conv_w, ffn_conv_b, ffn_w_out, norm_w):
    params = dict(hyb_w_in=hyb_w_in, ssd_conv_w=ssd_conv_w, ssd_conv_b=ssd_conv_b, ssd_a_log=ssd_a_log,
                  ssd_dt_bias=ssd_dt_bias, ssd_d=ssd_d, ssd_norm_w=ssd_norm_w,
                  mla_q_norm_w=mla_q_norm_w, mla_w_qb=mla_w_qb, mla_kv_norm_w=mla_kv_norm_w,
                  mla_w_kvb=mla_w_kvb, hyb_w_out=hyb_w_out,
                  gla_w_in=gla_w_in, gla_w_gk2=gla_w_gk2, gla_b_gk=gla_b_gk, gla_norm_w=gla_norm_w,
                  gla_w_out=gla_w_out, ffn_w_in=ffn_w_in, ffn_conv_w=ffn_conv_w, ffn_conv_b=ffn_conv_b,
                  ffn_w_out=ffn_w_out, norm_w=norm_w)
    assert x_prompt.shape[1:] == x_sample.shape[1:]
    nb = x_prompt.shape[0]
    y = _trunk(jnp.concatenate([x_prompt, x_sample], axis=0), params)
    return (y[:nb], y[nb:])
```

```python
import functools
import math

import jax
import jax.numpy as jnp
from jax import lax
from jax.experimental import pallas as pl
from jax.experimental.pallas import tpu as pltpu

F32 = jnp.float32
BF16 = jnp.bfloat16
LOG2_E = math.log2(math.e)

D_MODEL = 1024
EPS = 1e-6

SSD_HEADS = 16
SSD_HEAD_DIM = 64
SSD_INNER = SSD_HEADS * SSD_HEAD_DIM
SSD_GROUPS = 2
SSD_HPG = SSD_HEADS // SSD_GROUPS
SSD_STATE = 128
SSD_CHUNK = 128
SSD_CONV_DIM = SSD_INNER + 2 * SSD_GROUPS * SSD_STATE
SSD_GROUP_WIDTH = SSD_INNER // SSD_GROUPS

MLA_HEADS = 8
MLA_Q_LORA = 256
MLA_KV_LORA = 128
MLA_NOPE = 128
MLA_ROPE = 64
MLA_V = 128
ROPE_THETA = 10000.0
MLA_QK_WIDTH = 256
MLA_GROUP = 4

GLA_HEADS = 4
GLA_KEY = D_MODEL // 2
GLA_VAL = D_MODEL
GLA_HK = GLA_KEY // GLA_HEADS
GLA_HV = GLA_VAL // GLA_HEADS
GLA_GATE_RANK = 16
GLA_GATE_NORM = 16.0
GLA_CHUNK = 64

D_FF = 2816
FFN_COL_CHUNK = 256
OUT_ROW_BLOCK = 128

LANES = 128
SUBLANES = 8
HALO = 2 * SUBLANES
VMEM_LIMIT_BYTES = 56 * 1024 * 1024

_E_Z = 0
_E_XBC = _E_Z + SSD_INNER
_E_QA = _E_XBC + SSD_CONV_DIM
_E_CKV = _E_QA + MLA_Q_LORA
_E_KR = _E_CKV + MLA_KV_LORA
_E_KRSW = _E_KR + LANES
_E_DT = _E_KRSW + LANES
_E_END = _E_DT + LANES


def _dot(a, b):
    return jnp.dot(a, b, preferred_element_type=F32)


def _dot_nt(a, b):
    return lax.dot_general(a, b, (((1,), (1,)), ((), ())), preferred_element_type=F32)


def _dot_tn(a, b):
    return lax.dot_general(a, b, (((0,), (0,)), ((), ())), preferred_element_type=F32)


def _mask_dot(mask, v):
    t = mask.astype(BF16)
    h1 = v.astype(BF16)
    r1 = v - h1.astype(F32)
    h2 = r1.astype(BF16)
    h3 = (r1 - h2.astype(F32)).astype(BF16)
    if mask.shape[1] % LANES == 0:
        return _dot(jnp.concatenate([t, t, t], axis=1), jnp.concatenate([h1, h2, h3], axis=0))
    return _dot(t, h1) + _dot(t, h2) + _dot(t, h3)


def _rms(x, w):
    return x * lax.rsqrt(jnp.mean(x * x, axis=-1, keepdims=True) + EPS) * w


def _softplus(x):
    return jnp.maximum(x, 0.0) + jnp.log1p(jnp.exp(-jnp.abs(x)))


def _silu(x):
    return x * jax.nn.sigmoid(x)


def _gelu_tanh(x):
    k = -2.0 * math.sqrt(2.0 / math.pi) * LOG2_E
    return x / (1.0 + jnp.exp2((x * x * (k * 0.044715) + k) * x))


def _params(n_axes, flags=None):
    return pltpu.CompilerParams(dimension_semantics=("arbitrary",) * n_axes,
                                vmem_limit_bytes=VMEM_LIMIT_BYTES, flags=flags)


def _interleave(streams, skew=0):
    pending = list(enumerate(streams))
    rnd = 0
    while pending:
        alive = []
        for i, stream in pending:
            if rnd >= skew * i:
                try:
                    next(stream)
                except StopIteration:
                    continue
            alive.append((i, stream))
        pending = alive
        rnd += 1


def _const_spec(shape):
    nd = len(shape)
    return pl.BlockSpec(shape, lambda *_: (0,) * nd, pipeline_mode=pl.Buffered(1))


def _halo_specs(tm, seq):
    per = tm // SUBLANES
    last = seq // SUBLANES - 1
    main = pl.BlockSpec((1, tm, D_MODEL), lambda b, j: (b, j, 0))
    prev = pl.BlockSpec((1, SUBLANES, D_MODEL), lambda b, j: (b, jnp.maximum(j * per - 1, 0), 0))
    nxt = pl.BlockSpec((1, SUBLANES, D_MODEL), lambda b, j: (b, jnp.minimum((j + 1) * per, last), 0))
    return main, prev, nxt


def _fill_halo_rows(hcat_ref, xm_ref, xp_ref, xn_ref, nw, j, nt, tm):
    hn = jnp.where(j == nt - 1, 0.0, _rms(xn_ref[0], nw))
    hp = jnp.where(j == 0, 0.0, _rms(xp_ref[0], nw))
    hcat_ref[0:HALO, :] = jnp.concatenate([hn, hp], axis=0).astype(BF16)
    hcat_ref[HALO:, :] = _rms(xm_ref[0], nw).astype(BF16)


def _conv3(u, cw, tm):
    rows = tm + HALO
    um1 = pltpu.roll(u, 1, 0)[HALO:]
    up1 = pltpu.roll(u, rows - 1, 0)[HALO:]
    return um1 * cw[0:1] + u[HALO:] * cw[1:2] + up1 * cw[2:3] + cw[3:4]


def _even_in_kernel(xm_ref, xp_ref, xn_ref, nw_ref, wall_ref, cw_ref, dtb_ref, qnw_ref, wq_ref, wuk_ref,
                    kvnw_ref, cos_ref, sin_ref,
                    z_ref, xs_ref, bc_ref, dt_ref, q_ref, k_ref, hcat_ref, *, tm, nt):
    j = pl.program_id(1)
    _fill_halo_rows(hcat_ref, xm_ref, xp_ref, xn_ref, nw_ref[...], j, nt, tm)
    hcat = hcat_ref[...]
    hm = hcat_ref[HALO:, :]

    def ssd_columns():
        for c0 in range(0, SSD_CONV_DIM, 512):
            u = _dot(hcat, wall_ref[:, _E_XBC + c0:_E_XBC + c0 + 512])
            yield
            y = _silu(_conv3(u, cw_ref[:, c0:c0 + 512], tm)).astype(BF16)
            if c0 < SSD_INNER:
                xs_ref[0, :, c0:c0 + 512] = y
            else:
                bc_ref[0] = y
            yield

    def gate_columns():
        half = SSD_INNER // 2
        for c0 in (0, half):
            z = _dot(hm, wall_ref[:, _E_Z + c0:_E_Z + c0 + half])
            yield
            z_ref[0, :, c0:c0 + half] = z.astype(BF16)
            yield

    def mla_columns():
        rest = _dot(hm, wall_ref[:, _E_QA:_E_END])
        yield
        o = -_E_QA
        qa = rest[:, o + _E_QA:o + _E_CKV]
        ckv = rest[:, o + _E_CKV:o + _E_KR]
        kr = rest[:, o + _E_KR:o + _E_KRSW]
        krsw = rest[:, o + _E_KRSW:o + _E_DT]
        dtr = rest[:, o + _E_DT:o + _E_END]
        cos = cos_ref[...]
        sin = sin_ref[...]
        scale = (MLA_NOPE + MLA_ROPE) ** -0.5 * LOG2_E
        hq = _rms(qa, qnw_ref[...]).astype(BF16)
        nh = MLA_HEADS * LANES
        qall = _dot(hq, wq_ref[...])
        yield
        dt_ref[0] = _softplus(dtr + dtb_ref[...])
        k_ref[0, :, 0:LANES] = _rms(ckv, kvnw_ref[...]).astype(BF16)
        k_ref[0, :, LANES:2 * LANES] = (kr * cos + krsw * sin).astype(BF16)
        yield
        for h in range(MLA_HEADS):
            qn = qall[:, h * LANES:(h + 1) * LANES].astype(BF16)
            ql = _dot(qn, wuk_ref[h])
            qr = (qall[:, nh + h * LANES:nh + (h + 1) * LANES] * cos
                  + qall[:, 2 * nh + h * LANES:2 * nh + (h + 1) * LANES] * sin)
            q_ref[0, h, :, 0:LANES] = (ql * scale).astype(BF16)
            q_ref[0, h, :, LANES:MLA_QK_WIDTH] = (qr * scale).astype(BF16)
            if h % 2 == 1:
                yield

    _interleave([mla_columns(), ssd_columns(), gate_columns()])


def _even_in(x, nw, wall, cw, dtb, qnw, wq, wuk, kvnw, cos, sin, tm):
    bsz, seq, _ = x.shape
    nt = seq // tm
    main, prev, nxt = _halo_specs(tm, seq)

    def tok(width):
        return pl.BlockSpec((1, tm, width), lambda b, j: (b, j, 0))

    def out(width, dtype):
        return jax.ShapeDtypeStruct((bsz, seq, width), dtype)

    rope_spec = pl.BlockSpec((tm, LANES), lambda b, j: (j, 0))
    return pl.pallas_call(
        functools.partial(_even_in_kernel, tm=tm, nt=nt),
        grid=(bsz, nt),
        in_specs=[main, prev, nxt, _const_spec(nw.shape), _const_spec(wall.shape), _const_spec(cw.shape),
                  _const_spec(dtb.shape), _const_spec(qnw.shape), _const_spec(wq.shape), _const_spec(wuk.shape),
                  _const_spec(kvnw.shape), rope_spec, rope_spec],
        out_specs=[tok(SSD_INNER), tok(SSD_INNER), tok(2 * SSD_GROUPS * SSD_STATE), tok(LANES),
                   pl.BlockSpec((1, MLA_HEADS, tm, MLA_QK_WIDTH), lambda b, j: (b, 0, j, 0)), tok(MLA_QK_WIDTH)],
        out_shape=[out(SSD_INNER, BF16), out(SSD_INNER, BF16), out(2 * SSD_GROUPS * SSD_STATE, BF16),
                   out(LANES, F32), jax.ShapeDtypeStruct((bsz, MLA_HEADS, seq, MLA_QK_WIDTH), BF16),
                   out(MLA_QK_WIDTH, BF16)],
        scratch_shapes=[pltpu.VMEM((tm + HALO, D_MODEL), BF16)],
        compiler_params=_params(2),
        name="even_in",
    )(x, x, x, nw, wall, cw, dtb, qnw, wq, wuk, kvnw, cos, sin)


def _expand_heads(v, expand2):
    hi = v.astype(BF16)
    lo = (v - hi.astype(F32)).astype(BF16)
    return _dot(jnp.concatenate([hi, lo], axis=1), expand2)


def _ssd_stream(x_ref, bc_ref, dt_ref, y_ref, st_ref, a, reverse, lane0, nck):
    L = SSD_CHUNK
    row = lax.broadcasted_iota(jnp.int32, (L, L), 0)
    col = lax.broadcasted_iota(jnp.int32, (L, L), 1)
    tri = (row <= col) if reverse else (row >= col)
    er = lax.broadcasted_iota(jnp.int32, (LANES, SSD_INNER), 0)
    ec = lax.broadcasted_iota(jnp.int32, (LANES, SSD_INNER), 1)
    head_of_lane = lax.shift_right_logical(ec, int(math.log2(SSD_HEAD_DIM)))
    expand = jnp.where(er - lane0 == head_of_lane, 1.0, 0.0).astype(BF16)
    expand = jnp.concatenate([expand, expand], axis=0)
    lane = lax.broadcasted_iota(jnp.int32, (L, LANES), 1)
    low_half = lane < SSD_HEAD_DIM
    gw = SSD_GROUP_WIDTH
    groups = [slice(g * gw, (g + 1) * gw) for g in range(SSD_GROUPS)]
    staged = []
    for ci in (range(nck - 1, -1, -1) if reverse else range(nck)):
        rows = pl.ds(ci * L, L)
        dt = dt_ref[rows, :]
        cum = _mask_dot(tri, dt * a)
        yield
        cum_t = cum.T
        dt_t = dt.T
        tot = cum[0:1] if reverse else cum[L - 1:L]
        ecum = _expand_heads(jnp.exp(cum), expand)
        wend = _expand_heads(jnp.exp(tot - cum) * dt, expand)
        etot = _expand_heads(jnp.broadcast_to(jnp.exp(tot), (SUBLANES, LANES)), expand)[0:1]
        yield
        x = x_ref[rows, :]
        bc = bc_ref[rows, :]
        xw = (x.astype(F32) * wend).astype(BF16)
        bs = [bc[:, g * SSD_STATE:(g + 1) * SSD_STATE] for g in range(SSD_GROUPS)]
        cs = [bc[:, (SSD_GROUPS + g) * SSD_STATE:(SSD_GROUPS + g + 1) * SSD_STATE] for g in range(SSD_GROUPS)]
        cb = [_dot_nt(cs[g], bs[g]) for g in range(SSD_GROUPS)]
        upd = jnp.concatenate([_dot_tn(bs[g], xw[:, groups[g]]) for g in range(SSD_GROUPS)], axis=1)
        yield
        intra = []
        for g in range(SSD_GROUPS):
            for p in range(SSD_HPG // 2):
                lo = g * gw + p * LANES
                x_pair = x[:, lo:lo + LANES]
                zero = jnp.zeros_like(x_pair)
                x_stack = jnp.concatenate([jnp.where(low_half, x_pair, zero), jnp.where(low_half, zero, x_pair)],
                                          axis=0)
                pair = []
                for e in (2 * p, 2 * p + 1):
                    c = lane0 + g * SSD_HPG + e
                    diff = cum[:, c:c + 1] - cum_t[c:c + 1, :]
                    decay = jnp.exp(jnp.where(tri, diff, -jnp.inf))
                    pair.append((cb[g] * decay * dt_t[c:c + 1, :]).astype(BF16))
                intra.append(_dot(jnp.concatenate(pair, axis=1), x_stack))
            yield
        staged.append((rows, cs, ecum, etot, intra, upd))
    for rows, cs, ecum, etot, intra, upd in staged:
        st = st_ref[...]
        stb = st.astype(BF16)
        ys = []
        for g in range(SSD_GROUPS):
            y_inter = _dot(cs[g], stb[:, groups[g]])
            for p in range(SSD_HPG // 2):
                lo = g * gw + p * LANES
                ys.append(intra[g * (SSD_HPG // 2) + p] + y_inter[:, p * LANES:(p + 1) * LANES] * ecum[:, lo:lo + LANES])
        st_ref[...] = st * etot + upd
        yield
        y_ref[rows, :] = jnp.concatenate(ys, axis=1).astype(y_ref.dtype)
        yield


def _ssd_kernel(xf_ref, xb_ref, bcf_ref, bcb_ref, dtf_ref, dtb_ref, alog_ref, yf_ref, yb_ref, stf_ref, stb_ref,
                *, nck, nseq):
    @pl.when(pl.program_id(1) == 0)
    def _():
        stf_ref[...] = jnp.zeros_like(stf_ref)
        stb_ref[...] = jnp.zeros_like(stb_ref)

    a = -jnp.exp(alog_ref[...])
    streams = []
    for s in range(nseq):
        streams.append(_ssd_stream(xf_ref.at[s], bcf_ref.at[s], dtf_ref.at[s], yf_ref.at[s], stf_ref.at[s], a,
                                   False, 0, nck))
        streams.append(_ssd_stream(xb_ref.at[s], bcb_ref.at[s], dtb_ref.at[s], yb_ref.at[s], stb_ref.at[s], a,
                                   True, SSD_HEADS, nck))
    _interleave(streams)


def _ssd_scan(xs, bc, dt, alog, tb, nseq):
    bsz, seq, _ = xs.shape
    nb = seq // tb

    def fwd(width):
        return pl.BlockSpec((nseq, tb, width), lambda b, j: (b, j, 0))

    def bwd(width):
        return pl.BlockSpec((nseq, tb, width), lambda b, j: (b, nb - 1 - j, 0))

    out = jax.ShapeDtypeStruct((bsz, seq, SSD_INNER), BF16)
    state = pltpu.VMEM((nseq, SSD_STATE, SSD_INNER), F32)
    return pl.pallas_call(
        functools.partial(_ssd_kernel, nck=tb // SSD_CHUNK, nseq=nseq),
        grid=(bsz // nseq, nb),
        in_specs=[fwd(SSD_INNER), bwd(SSD_INNER), fwd(bc.shape[-1]), bwd(bc.shape[-1]), fwd(LANES), bwd(LANES),
                  _const_spec(alog.shape)],
        out_specs=[fwd(SSD_INNER), bwd(SSD_INNER)],
        out_shape=[out, out],
        scratch_shapes=[state, state],
        compiler_params=_params(2),
        name="ssd_scan",
    )(xs, xs, bc, bc, dt, dt, alog)


def _mla_kernel(q_ref, k_ref, wuv_ref, o_ref, *, tq):
    k = k_ref[0]
    v_t = jnp.concatenate([k[:, 0:MLA_KV_LORA].astype(F32).T.astype(BF16), jnp.ones((HALO, k.shape[0]), BF16)], axis=0)

    def group(g):
        q = q_ref[0, g * MLA_GROUP:(g + 1) * MLA_GROUP].reshape(MLA_GROUP * tq, MLA_QK_WIDTH)
        s = _dot_nt(q, k)
        yield
        m = jnp.max(s, axis=-1, keepdims=True)
        p = jnp.exp2(s - m).astype(BF16)
        o_t = _dot_nt(v_t, p)
        yield
        o_lat_t = o_t[0:MLA_KV_LORA] / o_t[MLA_KV_LORA:MLA_KV_LORA + 1]
        for i in range(MLA_GROUP):
            h = g * MLA_GROUP + i
            o_lat = o_lat_t[:, i * tq:(i + 1) * tq].T.astype(BF16)
            o_ref[0, :, h * MLA_V:(h + 1) * MLA_V] = _dot(o_lat, wuv_ref[h]).astype(o_ref.dtype)
        yield

    _interleave([group(g) for g in range(MLA_HEADS // MLA_GROUP)])


def _mla_attention(q, k, wuv, tq):
    bsz, _, seq, _ = q.shape
    return pl.pallas_call(
        functools.partial(_mla_kernel, tq=tq),
        grid=(bsz, seq // tq),
        in_specs=[pl.BlockSpec((1, MLA_HEADS, tq, MLA_QK_WIDTH), lambda b, j: (b, 0, j, 0)),
                  pl.BlockSpec((1, seq, k.shape[-1]), lambda b, j: (b, 0, 0)),
                  _const_spec(wuv.shape)],
        out_specs=pl.BlockSpec((1, tq, MLA_HEADS * MLA_V), lambda b, j: (b, j, 0)),
        out_shape=jax.ShapeDtypeStruct((bsz, seq, MLA_HEADS * MLA_V), BF16),
        compiler_params=_params(2),
        name="mla_attention",
    )(q, k, wuv)


def _even_out_kernel(yf_ref, yb_ref, xs_ref, z_ref, om_ref, x_ref, dskip_ref, snw_ref, wo_ref, nw_ref, o_ref):
    y = yf_ref[...].astype(F32) + yb_ref[...].astype(F32) + xs_ref[...].astype(F32) * dskip_ref[...]
    y = y * _silu(z_ref[...].astype(F32))
    snw = snw_ref[...]
    gw = SSD_GROUP_WIDTH
    m = _dot(om_ref[...], wo_ref[SSD_INNER:, :])
    for g in range(SSD_GROUPS):
        yg = _rms(y[:, g * gw:(g + 1) * gw], snw[:, g * gw:(g + 1) * gw]).astype(BF16)
        m = m + _dot(yg, wo_ref[g * gw:(g + 1) * gw, :])
    o_ref[...] = x_ref[...] + _rms(m, nw_ref[...])


def _even_out(yf, yb, xs, z, om, x, dskip, snw, wo, nw, tm):
    t = x.shape[0]
    tok = pl.BlockSpec((tm, D_MODEL), lambda i: (i, 0))
    return pl.pallas_call(
        _even_out_kernel,
        grid=(t // tm,),
        in_specs=[tok, tok, tok, tok, tok, tok, _const_spec(dskip.shape), _const_spec(snw.shape),
                  _const_spec(wo.shape), _const_spec(nw.shape)],
        out_specs=tok,
        out_shape=jax.ShapeDtypeStruct(x.shape, F32),
        compiler_params=_params(1),
        name="even_out",
    )(yf, yb, xs, z, om, x, dskip, snw, wo, nw)


def _gla_in_kernel(x_ref, nw_ref, w_ref, wgk_ref, bgk_ref, q_ref, k_ref, v_ref, g_ref, gk_ref):
    h = _rms(x_ref[...], nw_ref[...]).astype(BF16)
    o1, o2, o3, o4 = GLA_KEY, 2 * GLA_KEY, 2 * GLA_KEY + GLA_VAL, 2 * GLA_KEY + 2 * GLA_VAL

    def gates():
        lr = _dot(h, w_ref[:, o4:]).astype(BF16)
        yield
        pre = _dot(lr, wgk_ref[...]) + bgk_ref[...]
        yield
        gk_ref[...] = (jnp.minimum(pre, 0.0) - jnp.log1p(jnp.exp(-jnp.abs(pre)))) / GLA_GATE_NORM
        yield

    def projections():
        q_ref[...] = (_dot(h, w_ref[:, 0:o1]) * GLA_HK ** -0.5).astype(BF16)
        yield
        k_ref[...] = _dot(h, w_ref[:, o1:o2]).astype(BF16)
        yield
        v_ref[...] = _dot(h, w_ref[:, o2:o3]).astype(BF16)
        yield
        g_ref[...] = _dot(h, w_ref[:, o3:o4]).astype(BF16)
        yield

    _interleave([gates(), projections()])


def _gla_in(x, nw, w, wgk, bgk, tm):
    t = x.shape[0]

    def tok(width):
        return pl.BlockSpec((tm, width), lambda i: (i, 0))

    def out(width, dtype):
        return jax.ShapeDtypeStruct((t, width), dtype)

    return pl.pallas_call(
        _gla_in_kernel,
        grid=(t // tm,),
        in_specs=[tok(D_MODEL), _const_spec(nw.shape), _const_spec(w.shape), _const_spec(wgk.shape),
                  _const_spec(bgk.shape)],
        out_specs=[tok(GLA_KEY), tok(GLA_KEY), tok(GLA_VAL), tok(GLA_VAL), tok(2 * GLA_KEY)],
        out_shape=[out(GLA_KEY, BF16), out(GLA_KEY, BF16), out(GLA_VAL, BF16), out(GLA_VAL, BF16),
                   out(2 * GLA_KEY, F32)],
        compiler_params=_params(1),
        name="gla_in",
    )(x, nw, w, wgk, bgk)


def _gla_stream(q_ref, k_ref, v_ref, g_ref, o_ref, st_ref, reverse, nck):
    L = GLA_CHUNK
    row = lax.broadcasted_iota(jnp.int32, (L, L), 0)
    col = lax.broadcasted_iota(jnp.int32, (L, L), 1)
    tri = (row <= col) if reverse else (row >= col)
    keys = [slice(h * GLA_HK, (h + 1) * GLA_HK) for h in range(GLA_HEADS)]
    vals = [slice(h * GLA_HV, (h + 1) * GLA_HV) for h in range(GLA_HEADS)]
    staged = []
    for ci in (range(nck - 1, -1, -1) if reverse else range(nck)):
        rows = pl.ds(ci * L, L)
        cum = _mask_dot(tri, g_ref[rows, :])
        yield
        tot = cum[0:1] if reverse else cum[L - 1:L]
        qf = q_ref[rows, :].astype(F32)
        kf = k_ref[rows, :].astype(F32)
        qt = (qf * jnp.exp(cum)).astype(BF16)
        kt = (kf * jnp.exp(-cum)).astype(BF16)
        ke = (kf * jnp.exp(tot - cum)).astype(BF16)
        yield
        att = [_dot_nt(qt[:, ks], kt[:, ks]) for ks in keys]
        yield
        v = v_ref[rows, :]
        intra = [_dot(jnp.where(tri, att[h], 0.0).astype(BF16), v[:, vals[h]]) for h in range(GLA_HEADS)]
        upd = [_dot_tn(v[:, vals[h]], ke[:, keys[h]]) for h in range(GLA_HEADS)]
        yield
        staged.append((rows, qt, jnp.exp(tot), intra, jnp.concatenate(upd, axis=1)))
    for rows, qt, etot, intra, upd in staged:
        st = st_ref[...]
        stb = st.astype(BF16)
        outs = [intra[h] + _dot_nt(qt[:, keys[h]], stb[:, keys[h]]) for h in range(GLA_HEADS)]
        st_ref[...] = st * etot + upd
        yield
        o_ref[rows, :] = jnp.concatenate(outs, axis=1).astype(o_ref.dtype)
        yield


def _gla_kernel(qf_ref, qb_ref, kf_ref, kb_ref, vf_ref, vb_ref, gf_ref, gb_ref, of_ref, ob_ref, stf_ref, stb_ref,
                *, nck, nseq):
    @pl.when(pl.program_id(1) == 0)
    def _():
        stf_ref[...] = jnp.zeros_like(stf_ref)
        stb_ref[...] = jnp.zeros_like(stb_ref)

    streams = []
    for s in range(nseq):
        streams.append(_gla_stream(qf_ref.at[s], kf_ref.at[s], vf_ref.at[s], gf_ref.at[s], of_ref.at[s],
                                   stf_ref.at[s], False, nck))
        streams.append(_gla_stream(qb_ref.at[s], kb_ref.at[s], vb_ref.at[s], gb_ref.at[s], ob_ref.at[s],
                                   stb_ref.at[s], True, nck))
    _interleave(streams)


def _gla_scan(q, k, v, gk, tb, nseq):
    bsz, seq, _ = q.shape
    nb = seq // tb

    def fwd(width, cb=0):
        return pl.BlockSpec((nseq, tb, width), lambda b, j: (b, j, cb))

    def bwd(width, cb=0):
        return pl.BlockSpec((nseq, tb, width), lambda b, j: (b, nb - 1 - j, cb))

    out = jax.ShapeDtypeStruct((bsz, seq, GLA_VAL), BF16)
    state = pltpu.VMEM((nseq, GLA_HV, GLA_KEY), F32)
    return pl.pallas_call(
        functools.partial(_gla_kernel, nck=tb // GLA_CHUNK, nseq=nseq),
        grid=(bsz // nseq, nb),
        in_specs=[fwd(GLA_KEY), bwd(GLA_KEY), fwd(GLA_KEY), bwd(GLA_KEY), fwd(GLA_VAL), bwd(GLA_VAL),
                  fwd(GLA_KEY, 0), bwd(GLA_KEY, 1)],
        out_specs=[fwd(GLA_VAL), bwd(GLA_VAL)],
        out_shape=[out, out],
        scratch_shapes=[state, state],
        compiler_params=_params(2),
        name="gla_scan",
    )(q, q, k, k, v, v, gk, gk)


def _gla_out_kernel(of_ref, ob_ref, g_ref, x_ref, gnw_ref, wo_ref, nw_ref, o_ref):
    o = of_ref[...].astype(F32) + ob_ref[...].astype(F32)
    gate = _silu(g_ref[...].astype(F32))
    gnw = gnw_ref[...]
    m = None
    for h in range(GLA_HEADS):
        hs = slice(h * GLA_HV, (h + 1) * GLA_HV)
        oh = (_rms(o[:, hs], gnw) * gate[:, hs]).astype(BF16)
        d = _dot(oh, wo_ref[hs, :])
        m = d if m is None else m + d
    o_ref[...] = x_ref[...] + _rms(m, nw_ref[...])


def _gla_out(of, ob, g, x, gnw, wo, nw, tm):
    t = x.shape[0]
    tok = pl.BlockSpec((tm, D_MODEL), lambda i: (i, 0))
    return pl.pallas_call(
        _gla_out_kernel,
        grid=(t // tm,),
        in_specs=[tok, tok, tok, tok, _const_spec(gnw.shape), _const_spec(wo.shape), _const_spec(nw.shape)],
        out_specs=tok,
        out_shape=jax.ShapeDtypeStruct(x.shape, F32),
        compiler_params=_params(1),
        name="gla_out",
    )(of, ob, g, x, gnw, wo, nw)


def _ffn_kernel(xm_ref, xp_ref, xn_ref, nw_ref, wg_ref, wu_ref, cwg_ref, cwu_ref, wo_ref, nwo_ref, o_ref,
                hcat_ref, *, tm, nt, nchunk):
    j = pl.program_id(1)
    _fill_halo_rows(hcat_ref, xm_ref, xp_ref, xn_ref, nw_ref[...], j, nt, tm)
    hcat = hcat_ref[...]

    def project(c):
        return _dot(hcat, wg_ref[c]), _dot(hcat, wu_ref[c])

    u_next = project(0)
    acc = None
    acts = []
    for c in range(nchunk):
        u_gate, u_up = u_next
        if c + 1 < nchunk:
            u_next = project(c + 1)
        gate = _conv3(u_gate, cwg_ref[c], tm)
        up = _conv3(u_up, cwu_ref[c], tm)
        acts.append((_gelu_tanh(gate) * up).astype(BF16))
        if len(acts) == 2 or c + 1 == nchunk:
            c0 = c + 1 - len(acts)
            act = acts[0] if len(acts) == 1 else jnp.concatenate(acts, axis=1)
            d = _dot(act, wo_ref[c0 * FFN_COL_CHUNK:(c + 1) * FFN_COL_CHUNK, :])
            acc = d if acc is None else acc + d
            acts = []
    o_ref[0] = xm_ref[0] + _rms(acc, nwo_ref[...])


def _ffn(x, nw, wg, wu, cwg, cwu, wo, nwo, tm):
    bsz, seq, _ = x.shape
    nt = seq // tm
    main, prev, nxt = _halo_specs(tm, seq)
    return pl.pallas_call(
        functools.partial(_ffn_kernel, tm=tm, nt=nt, nchunk=wg.shape[0]),
        grid=(bsz, nt),
        in_specs=[main, prev, nxt, _const_spec(nw.shape), _const_spec(wg.shape), _const_spec(wu.shape),
                  _const_spec(cwg.shape), _const_spec(cwu.shape), _const_spec(wo.shape), _const_spec(nwo.shape)],
        out_specs=pl.BlockSpec((1, tm, D_MODEL), lambda b, j: (b, j, 0)),
        out_shape=jax.ShapeDtypeStruct(x.shape, F32),
        scratch_shapes=[pltpu.VMEM((tm + HALO, D_MODEL), BF16)],
        compiler_params=_params(2),
        name="conv_ffn",
    )(x, x, x, nw, wg, wu, cwg, cwu, wo, nwo)


def _pad_cols(w, width):
    return jnp.pad(w, ((0, 0),) * (w.ndim - 1) + ((0, width - w.shape[-1]),))


def _swap_halves(w):
    half = w.shape[-1] // 2
    return jnp.concatenate([w[..., half:], w[..., :half]], axis=-1)


def _row(v):
    return v.reshape(1, -1).astype(F32)


def _prep_even(w_in, conv_w, conv_b, a_log, dt_bias, d_skip, ssd_norm_w, q_norm_w, w_qb, kv_norm_w, w_kvb, w_out):
    o1 = SSD_INNER
    o2 = o1 + SSD_CONV_DIM
    o3 = o2 + 2 * SSD_HEADS
    o4 = o3 + MLA_Q_LORA
    o5 = o4 + MLA_KV_LORA
    w_kr = w_in[:, o5:]
    wall = jnp.concatenate([w_in[:, :o2], w_in[:, o3:o5], _pad_cols(w_kr, LANES), _pad_cols(_swap_halves(w_kr), LANES),
                            _pad_cols(w_in[:, o2:o3], LANES)], axis=1).astype(BF16)
    cw = jnp.concatenate([conv_w, conv_b[None, :]], axis=0).astype(F32)
    dtb = _pad_cols(_row(dt_bias), LANES)
    alog = _pad_cols(_row(a_log), LANES)
    dskip = _row(jnp.repeat(d_skip, SSD_HEAD_DIM))
    wq3 = w_qb.reshape(MLA_Q_LORA, MLA_HEADS, MLA_NOPE + MLA_ROPE)
    rope = wq3[..., MLA_NOPE:]
    wq = jnp.concatenate([wq3[..., :MLA_NOPE].reshape(MLA_Q_LORA, -1),
                          _pad_cols(rope, LANES).reshape(MLA_Q_LORA, -1),
                          _pad_cols(_swap_halves(rope), LANES).reshape(MLA_Q_LORA, -1)], axis=1).astype(BF16)
    wkv3 = w_kvb.reshape(MLA_KV_LORA, MLA_HEADS, MLA_NOPE + MLA_V)
    wuk = jnp.transpose(wkv3[..., :MLA_NOPE], (1, 2, 0)).astype(BF16)
    wuv = jnp.transpose(wkv3[..., MLA_NOPE:], (1, 0, 2)).astype(BF16)
    return dict(wall=wall, cw=cw, dtb=dtb, alog=alog, dskip=dskip, snw=_row(ssd_norm_w), qnw=_row(q_norm_w), wq=wq,
                wuk=wuk, kvnw=_row(kv_norm_w), wuv=wuv, wo=w_out.astype(BF16))


def _rope_tables(seq):
    inv = 1.0 / (ROPE_THETA ** (jnp.arange(0, MLA_ROPE, 2, dtype=F32) / MLA_ROPE))
    ang = jnp.arange(seq, dtype=F32)[:, None] * inv[None, :]
    cos, sin = jnp.cos(ang), jnp.sin(ang)
    return (_pad_cols(jnp.concatenate([cos, cos], axis=1), LANES),
            _pad_cols(jnp.concatenate([-sin, sin], axis=1), LANES))


def _prep_gla(w_in, w_gk2, b_gk, norm_w, w_out):
    w = _pad_cols(w_in, 2 * GLA_KEY + 2 * GLA_VAL + LANES).astype(BF16)
    wgk = jnp.zeros((LANES, 2 * GLA_KEY), F32)
    for d in range(2):
        wgk = wgk.at[d * GLA_GATE_RANK:(d + 1) * GLA_GATE_RANK, d * GLA_KEY:(d + 1) * GLA_KEY].set(w_gk2[d])
    return dict(w=w, wgk=wgk.astype(BF16), bgk=_row(b_gk), gnw=_row(norm_w), wo=w_out.astype(BF16))


def _prep_ffn(w_in, conv_w, conv_b, w_out):
    nchunk = D_FF // FFN_COL_CHUNK

    def cols(w):
        return jnp.transpose(w.reshape(w.shape[0], nchunk, FFN_COL_CHUNK), (1, 0, 2))

    cw = jnp.concatenate([conv_w, conv_b[None, :], jnp.zeros((4, 2 * D_FF), F32)], axis=0).astype(F32)
    return dict(wg=cols(w_in[:, :D_FF]).astype(BF16), wu=cols(w_in[:, D_FF:]).astype(BF16),
                cwg=cols(cw[:, :D_FF]), cwu=cols(cw[:, D_FF:]),
                wo=w_out.astype(BF16))


def _tile(total, pref):
    return min(total, pref)


def _trunk(x, p, tiles=None):
    tiles = dict(dict(even_in=512, ssd=512, ssd_seqs=1, mla=256, tok=512, gla=256, gla_seqs=2, ffn=512), **(tiles or {}))
    bsz, seq, d = x.shape
    depth = p['norm_w'].shape[0]
    cos, sin = _rope_tables(seq)
    flat = lambda t: t.reshape(bsz * seq, t.shape[-1])
    for layer in range(depth):
        i = layer // 2
        nw = p['norm_w'][layer].astype(F32)
        if layer % 2 == 0:
            e = _prep_even(p['hyb_w_in'][i], p['ssd_conv_w'][i], p['ssd_conv_b'][i], p['ssd_a_log'][i],
                           p['ssd_dt_bias'][i], p['ssd_d'][i], p['ssd_norm_w'][i], p['mla_q_norm_w'][i],
                           p['mla_w_qb'][i], p['mla_kv_norm_w'][i], p['mla_w_kvb'][i], p['hyb_w_out'][i])
            z, xs, bc, dt, q, k = _even_in(x, nw[0:1], e['wall'], e['cw'], e['dtb'], e['qnw'], e['wq'], e['wuk'],
                                           e['kvnw'], cos, sin, _tile(seq, tiles['even_in']))
            yf, yb = _ssd_scan(xs, bc, dt, e['alog'], _tile(seq, tiles['ssd']), math.gcd(bsz, tiles['ssd_seqs']))
            om = _mla_attention(q, k, e['wuv'], _tile(seq, tiles['mla']))
            x = _even_out(flat(yf), flat(yb), flat(xs), flat(z), flat(om), flat(x), e['dskip'], e['snw'], e['wo'],
                          nw[1:2], _tile(bsz * seq, tiles['tok'])).reshape(bsz, seq, d)
        else:
            gp = _prep_gla(p['gla_w_in'][i], p['gla_w_gk2'][i], p['gla_b_gk'][i], p['gla_norm_w'][i], p['gla_w_out'][i])
            q, k, v, g, gk = _gla_in(flat(x), nw[0:1], gp['w'], gp['wgk'], gp['bgk'], _tile(bsz * seq, tiles['tok']))
            r3 = lambda t: t.reshape(bsz, seq, t.shape[-1])
            of, ob = _gla_scan(r3(q), r3(k), r3(v), r3(gk), _tile(seq, tiles['gla']), math.gcd(bsz, tiles['gla_seqs']))
            x = _gla_out(flat(of), flat(ob), g, flat(x), gp['gnw'], gp['wo'], nw[1:2],
                         _tile(bsz * seq, tiles['tok'])).reshape(bsz, seq, d)
        f = _prep_ffn(p['ffn_w_in'][layer], p['ffn_conv_w'][layer], p['ffn_conv_b'][layer], p['ffn_w_out'][layer])
        x = _ffn(x, nw[2:3], f['wg'], f['wu'], f['cwg'], f['cwu'], f['wo'], nw[3:4], _tile(seq, tiles['ffn']))
    return x


def kernel(x_prompt, x_sample, hyb_w_in, ssd_conv_w, ssd_conv_b, ssd_a_log, ssd_dt_bias, ssd_d, ssd_norm_w, mla_q_norm_w, mla_w_qb, mla_kv_norm_w, mla_w_kvb, hyb_w_out, gla_w_in, gla_w_gk2, gla_b_gk, gla_norm_w, gla_w_out, ffn_w_in, ffn_conv_w, ffn_conv_b, ffn_w_out, norm_w):
    params = dict(hyb_w_in=hyb_w_in, ssd_conv_w=ssd_conv_w, ssd_conv_b=ssd_conv_b, ssd_a_log=ssd_a_log,
                  ssd_dt_bias=ssd_dt_bias, ssd_d=ssd_d, ssd_norm_w=ssd_norm_w,
                  mla_q_norm_w=mla_q_norm_w, mla_w_qb=mla_w_qb, mla_kv_norm_w=mla_kv_norm_w,
                  mla_w_kvb=mla_w_kvb, hyb_w_out=hyb_w_out,
                  gla_w_in=gla_w_in, gla_w_gk2=gla_w_gk2, gla_b_gk=gla_b_gk, gla_norm_w=gla_norm_w,
                  gla_w_out=gla_w_out, ffn_w_in=ffn_w_in, ffn_conv_w=ffn_conv_w, ffn_conv_b=ffn_conv_b,
                  ffn_w_out=ffn_w_out, norm_w=norm_w)
    assert x_prompt.shape[1:] == x_sample.shape[1:]
    nb = x_prompt.shape[0]
    y = _trunk(jnp.concatenate([x_prompt, x_sample], axis=0), params)
    return (y[:nb], y[nb:])
```

```python
import functools
import math

import jax
import jax.numpy as jnp
from jax import lax
from jax.experimental import pallas as pl
from jax.experimental.pallas import tpu as pltpu

F32 = jnp.float32
BF16 = jnp.bfloat16
LOG2_E = math.log2(math.e)

D_MODEL = 1024
EPS = 1e-6

SSD_HEADS = 16
SSD_HEAD_DIM = 64
SSD_INNER = SSD_HEADS * SSD_HEAD_DIM
SSD_GROUPS = 2
SSD_HPG = SSD_HEADS // SSD_GROUPS
SSD_STATE = 128
SSD_CHUNK = 128
SSD_CONV_DIM = SSD_INNER + 2 * SSD_GROUPS * SSD_STATE
SSD_GROUP_WIDTH = SSD_INNER // SSD_GROUPS

MLA_HEADS = 8
MLA_Q_LORA = 256
MLA_KV_LORA = 128
MLA_NOPE = 128
MLA_ROPE = 64
MLA_V = 128
ROPE_THETA = 10000.0
MLA_QK_WIDTH = 256
MLA_GROUP = 4

GLA_HEADS = 4
GLA_KEY = D_MODEL // 2
GLA_VAL = D_MODEL
GLA_HK = GLA_KEY // GLA_HEADS
GLA_HV = GLA_VAL // GLA_HEADS
GLA_GATE_RANK = 16
GLA_GATE_NORM = 16.0
GLA_CHUNK = 64

D_FF = 2816
FFN_COL_CHUNK = 256

LANES = 128
SUBLANES = 8
HALO = 2 * SUBLANES
VMEM_LIMIT_BYTES = 56 * 1024 * 1024

_E_Z = 0
_E_XBC = _E_Z + SSD_INNER
_E_QA = _E_XBC + SSD_CONV_DIM
_E_CKV = _E_QA + MLA_Q_LORA
_E_KR = _E_CKV + MLA_KV_LORA
_E_KRSW = _E_KR + LANES
_E_DT = _E_KRSW + LANES
_E_END = _E_DT + LANES


def _dot(a, b):
    return jnp.dot(a, b, preferred_element_type=F32)


def _dot_nt(a, b):
    return lax.dot_general(a, b, (((1,), (1,)), ((), ())), preferred_element_type=F32)


def _dot_tn(a, b):
    return lax.dot_general(a, b, (((0,), (0,)), ((), ())), preferred_element_type=F32)


def _mask_dot(mask, v):
    t = mask.astype(BF16)
    h1 = v.astype(BF16)
    r1 = v - h1.astype(F32)
    h2 = r1.astype(BF16)
    h3 = (r1 - h2.astype(F32)).astype(BF16)
    if mask.shape[1] % LANES == 0:
        return _dot(jnp.concatenate([t, t, t], axis=1), jnp.concatenate([h1, h2, h3], axis=0))
    return _dot(t, h1) + _dot(t, h2) + _dot(t, h3)


def _rms(x, w):
    return x * lax.rsqrt(jnp.mean(x * x, axis=-1, keepdims=True) + EPS) * w


def _softplus(x):
    return jnp.maximum(x, 0.0) + jnp.log1p(jnp.exp(-jnp.abs(x)))


def _silu(x):
    return x * jax.nn.sigmoid(x)


def _gelu_tanh(x):
    k = -2.0 * math.sqrt(2.0 / math.pi) * LOG2_E
    return x / (1.0 + jnp.exp2((x * x * (k * 0.044715) + k) * x))


def _params(n_axes, flags=None):
    return pltpu.CompilerParams(dimension_semantics=("arbitrary",) * n_axes,
                                vmem_limit_bytes=VMEM_LIMIT_BYTES, flags=flags)


def _interleave(streams, skew=0):
    pending = list(enumerate(streams))
    rnd = 0
    while pending:
        alive = []
        for i, stream in pending:
            if rnd >= skew * i:
                try:
                    next(stream)
                except StopIteration:
                    continue
            alive.append((i, stream))
        pending = alive
        rnd += 1


def _const_spec(shape):
    nd = len(shape)
    return pl.BlockSpec(shape, lambda *_: (0,) * nd, pipeline_mode=pl.Buffered(1))


def _halo_specs(tm, seq):
    per = tm // SUBLANES
    last = seq // SUBLANES - 1
    main = pl.BlockSpec((1, tm, D_MODEL), lambda b, j: (b, j, 0))
    prev = pl.BlockSpec((1, SUBLANES, D_MODEL), lambda b, j: (b, jnp.maximum(j * per - 1, 0), 0))
    nxt = pl.BlockSpec((1, SUBLANES, D_MODEL), lambda b, j: (b, jnp.minimum((j + 1) * per, last), 0))
    return main, prev, nxt


def _fill_halo_rows(hcat_ref, xm_ref, xp_ref, xn_ref, nw, j, nt, tm):
    hn = jnp.where(j == nt - 1, 0.0, _rms(xn_ref[0], nw))
    hp = jnp.where(j == 0, 0.0, _rms(xp_ref[0], nw))
    hcat_ref[0:HALO, :] = jnp.concatenate([hn, hp], axis=0).astype(BF16)
    hcat_ref[HALO:, :] = _rms(xm_ref[0], nw).astype(BF16)


def _conv3(u, cw, tm):
    rows = tm + HALO
    um1 = pltpu.roll(u, 1, 0)[HALO:]
    up1 = pltpu.roll(u, rows - 1, 0)[HALO:]
    return um1 * cw[0:1] + u[HALO:] * cw[1:2] + up1 * cw[2:3] + cw[3:4]


def _even_in_kernel(xm_ref, xp_ref, xn_ref, nw_ref, wall_ref, cw_ref, dtb_ref, qnw_ref, wq_ref, wuk_ref,
                    kvnw_ref, cos_ref, sin_ref,
                    z_ref, xs_ref, bc_ref, dt_ref, q_ref, k_ref, hcat_ref, *, tm, nt):
    j = pl.program_id(1)
    _fill_halo_rows(hcat_ref, xm_ref, xp_ref, xn_ref, nw_ref[...], j, nt, tm)
    hcat = hcat_ref[...]
    hm = hcat_ref[HALO:, :]

    def ssd_columns():
        for c0 in range(0, SSD_CONV_DIM, 512):
            u = _dot(hcat, wall_ref[:, _E_XBC + c0:_E_XBC + c0 + 512])
            yield
            y = _silu(_conv3(u, cw_ref[:, c0:c0 + 512], tm)).astype(BF16)
            if c0 < SSD_INNER:
                xs_ref[0, :, c0:c0 + 512] = y
            else:
                bc_ref[0] = y
            yield

    def gate_columns():
        half = SSD_INNER // 2
        for c0 in (0, half):
            z = _dot(hm, wall_ref[:, _E_Z + c0:_E_Z + c0 + half])
            yield
            z_ref[0, :, c0:c0 + half] = z.astype(BF16)
            yield

    def mla_columns():
        rest = _dot(hm, wall_ref[:, _E_QA:_E_END])
        yield
        o = -_E_QA
        qa = rest[:, o + _E_QA:o + _E_CKV]
        ckv = rest[:, o + _E_CKV:o + _E_KR]
        kr = rest[:, o + _E_KR:o + _E_KRSW]
        krsw = rest[:, o + _E_KRSW:o + _E_DT]
        dtr = rest[:, o + _E_DT:o + _E_END]
        cos = cos_ref[...]
        sin = sin_ref[...]
        scale = (MLA_NOPE + MLA_ROPE) ** -0.5 * LOG2_E
        hq = _rms(qa, qnw_ref[...]).astype(BF16)
        nh = MLA_HEADS * LANES
        qall = _dot(hq, wq_ref[...])
        yield
        dt_ref[0] = _softplus(dtr + dtb_ref[...])
        k_ref[0, :, 0:LANES] = _rms(ckv, kvnw_ref[...]).astype(BF16)
        k_ref[0, :, LANES:2 * LANES] = (kr * cos + krsw * sin).astype(BF16)
        yield
        for h in range(MLA_HEADS):
            qn = qall[:, h * LANES:(h + 1) * LANES].astype(BF16)
            ql = _dot(qn, wuk_ref[h])
            qr = (qall[:, nh + h * LANES:nh + (h + 1) * LANES] * cos
                  + qall[:, 2 * nh + h * LANES:2 * nh + (h + 1) * LANES] * sin)
            q_ref[0, h, :, 0:LANES] = (ql * scale).astype(BF16)
            q_ref[0, h, :, LANES:MLA_QK_WIDTH] = (qr * scale).astype(BF16)
            if h % 2 == 1:
                yield

    _interleave([mla_columns(), ssd_columns(), gate_columns()])


def _even_in(x, nw, wall, cw, dtb, qnw, wq, wuk, kvnw, cos, sin, tm):
    bsz, seq, _ = x.shape
    nt = seq // tm
    main, prev, nxt = _halo_specs(tm, seq)

    def tok(width):
        return pl.BlockSpec((1, tm, width), lambda b, j: (b, j, 0))

    def out(width, dtype):
        return jax.ShapeDtypeStruct((bsz, seq, width), dtype)

    rope_spec = pl.BlockSpec((tm, LANES), lambda b, j: (j, 0))
    return pl.pallas_call(
        functools.partial(_even_in_kernel, tm=tm, nt=nt),
        grid=(bsz, nt),
        in_specs=[main, prev, nxt, _const_spec(nw.shape), _const_spec(wall.shape), _const_spec(cw.shape),
                  _const_spec(dtb.shape), _const_spec(qnw.shape), _const_spec(wq.shape), _const_spec(wuk.shape),
                  _const_spec(kvnw.shape), rope_spec, rope_spec],
        out_specs=[tok(SSD_INNER), tok(SSD_INNER), tok(2 * SSD_GROUPS * SSD_STATE), tok(LANES),
                   pl.BlockSpec((1, MLA_HEADS, tm, MLA_QK_WIDTH), lambda b, j: (b, 0, j, 0)), tok(MLA_QK_WIDTH)],
        out_shape=[out(SSD_INNER, BF16), out(SSD_INNER, BF16), out(2 * SSD_GROUPS * SSD_STATE, BF16),
                   out(LANES, F32), jax.ShapeDtypeStruct((bsz, MLA_HEADS, seq, MLA_QK_WIDTH), BF16),
                   out(MLA_QK_WIDTH, BF16)],
        scratch_shapes=[pltpu.VMEM((tm + HALO, D_MODEL), BF16)],
        compiler_params=_params(2),
        name="even_in",
    )(x, x, x, nw, wall, cw, dtb, qnw, wq, wuk, kvnw, cos, sin)


def _expand_heads(v, expand2):
    hi = v.astype(BF16)
    lo = (v - hi.astype(F32)).astype(BF16)
    return _dot(jnp.concatenate([hi, lo], axis=1), expand2)


def _ssd_stream(x_ref, bc_ref, dt_ref, y_ref, st_ref, a, reverse, lane0, nck):
    L = SSD_CHUNK
    row = lax.broadcasted_iota(jnp.int32, (L, L), 0)
    col = lax.broadcasted_iota(jnp.int32, (L, L), 1)
    tri = (row <= col) if reverse else (row >= col)
    er = lax.broadcasted_iota(jnp.int32, (LANES, SSD_INNER), 0)
    ec = lax.broadcasted_iota(jnp.int32, (LANES, SSD_INNER), 1)
    head_of_lane = lax.shift_right_logical(ec, int(math.log2(SSD_HEAD_DIM)))
    expand = jnp.where(er - lane0 == head_of_lane, 1.0, 0.0).astype(BF16)
    expand = jnp.concatenate([expand, expand], axis=0)
    lane = lax.broadcasted_iota(jnp.int32, (L, LANES), 1)
    low_half = lane < SSD_HEAD_DIM
    gw = SSD_GROUP_WIDTH
    groups = [slice(g * gw, (g + 1) * gw) for g in range(SSD_GROUPS)]
    staged = []
    for ci in (range(nck - 1, -1, -1) if reverse else range(nck)):
        rows = pl.ds(ci * L, L)
        dt = dt_ref[rows, :]
        cum = _mask_dot(tri, dt * a)
        yield
        cum_t = cum.T
        dt_t = dt.T
        tot = cum[0:1] if reverse else cum[L - 1:L]
        ecum = _expand_heads(jnp.exp(cum), expand)
        wend = _expand_heads(jnp.exp(tot - cum) * dt, expand)
        etot = _expand_heads(jnp.broadcast_to(jnp.exp(tot), (SUBLANES, LANES)), expand)[0:1]
        yield
        x = x_ref[rows, :]
        bc = bc_ref[rows, :]
        xw = (x.astype(F32) * wend).astype(BF16)
        bs = [bc[:, g * SSD_STATE:(g + 1) * SSD_STATE] for g in range(SSD_GROUPS)]
        cs = [bc[:, (SSD_GROUPS + g) * SSD_STATE:(SSD_GROUPS + g + 1) * SSD_STATE] for g in range(SSD_GROUPS)]
        cb = [_dot_nt(cs[g], bs[g]) for g in range(SSD_GROUPS)]
        upd = jnp.concatenate([_dot_tn(bs[g], xw[:, groups[g]]) for g in range(SSD_GROUPS)], axis=1)
        yield
        intra = []
        for g in range(SSD_GROUPS):
            for p in range(SSD_HPG // 2):
                lo = g * gw + p * LANES
                x_pair = x[:, lo:lo + LANES]
                zero = jnp.zeros_like(x_pair)
                x_stack = jnp.concatenate([jnp.where(low_half, x_pair, zero), jnp.where(low_half, zero, x_pair)],
                                          axis=0)
                pair = []
                for e in (2 * p, 2 * p + 1):
                    c = lane0 + g * SSD_HPG + e
                    diff = cum[:, c:c + 1] - cum_t[c:c + 1, :]
                    decay = jnp.exp(jnp.where(tri, diff, -jnp.inf))
                    pair.append((cb[g] * decay * dt_t[c:c + 1, :]).astype(BF16))
                intra.append(_dot(jnp.concatenate(pair, axis=1), x_stack))
            yield
        staged.append((rows, cs, ecum, etot, intra, upd))
    for rows, cs, ecum, etot, intra, upd in staged:
        st = st_ref[...]
        stb = st.astype(BF16)
        ys = []
        for g in range(SSD_GROUPS):
            y_inter = _dot(cs[g], stb[:, groups[g]])
            for p in range(SSD_HPG // 2):
                lo = g * gw + p * LANES
                ys.append(intra[g * (SSD_HPG // 2) + p] + y_inter[:, p * LANES:(p + 1) * LANES] * ecum[:, lo:lo + LANES])
        st_ref[...] = st * etot + upd
        yield
        y_ref[rows, :] = jnp.concatenate(ys, axis=1).astype(y_ref.dtype)
        yield


def _ssd_kernel(xf_ref, xb_ref, bcf_ref, bcb_ref, dtf_ref, dtb_ref, alog_ref, yf_ref, yb_ref, stf_ref, stb_ref,
                *, nck, nseq):
    @pl.when(pl.program_id(1) == 0)
    def _():
        stf_ref[...] = jnp.zeros_like(stf_ref)
        stb_ref[...] = jnp.zeros_like(stb_ref)

    a = -jnp.exp(alog_ref[...])
    streams = []
    for s in range(nseq):
        streams.append(_ssd_stream(xf_ref.at[s], bcf_ref.at[s], dtf_ref.at[s], yf_ref.at[s], stf_ref.at[s], a,
                                   False, 0, nck))
        streams.append(_ssd_stream(xb_ref.at[s], bcb_ref.at[s], dtb_ref.at[s], yb_ref.at[s], stb_ref.at[s], a,
                                   True, SSD_HEADS, nck))
    _interleave(streams)


def _ssd_scan(xs, bc, dt, alog, tb, nseq):
    bsz, seq, _ = xs.shape
    nb = seq // tb

    def fwd(width):
        return pl.BlockSpec((nseq, tb, width), lambda b, j: (b, j, 0))

    def bwd(width):
        return pl.BlockSpec((nseq, tb, width), lambda b, j: (b, nb - 1 - j, 0))

    out = jax.ShapeDtypeStruct((bsz, seq, SSD_INNER), BF16)
    state = pltpu.VMEM((nseq, SSD_STATE, SSD_INNER), F32)
    return pl.pallas_call(
        functools.partial(_ssd_kernel, nck=tb // SSD_CHUNK, nseq=nseq),
        grid=(bsz // nseq, nb),
        in_specs=[fwd(SSD_INNER), bwd(SSD_INNER), fwd(bc.shape[-1]), bwd(bc.shape[-1]), fwd(LANES), bwd(LANES),
                  _const_spec(alog.shape)],
        out_specs=[fwd(SSD_INNER), bwd(SSD_INNER)],
        out_shape=[out, out],
        scratch_shapes=[state, state],
        compiler_params=_params(2),
        name="ssd_scan",
    )(xs, xs, bc, bc, dt, dt, alog)


def _mla_kernel(q_ref, k_ref, wuv_ref, o_ref, *, tq):
    k = k_ref[0]
    v_t = jnp.concatenate([k[:, 0:MLA_KV_LORA].astype(F32).T.astype(BF16), jnp.ones((HALO, k.shape[0]), BF16)], axis=0)

    def group(g):
        q = q_ref[0, g * MLA_GROUP:(g + 1) * MLA_GROUP].reshape(MLA_GROUP * tq, MLA_QK_WIDTH)
        s = _dot_nt(q, k)
        yield
        m = jnp.max(s, axis=-1, keepdims=True)
        p = jnp.exp2(s - m).astype(BF16)
        o_t = _dot_nt(v_t, p)
        yield
        o_lat_t = o_t[0:MLA_KV_LORA] / o_t[MLA_KV_LORA:MLA_KV_LORA + 1]
        for i in range(MLA_GROUP):
            h = g * MLA_GROUP + i
            o_lat = o_lat_t[:, i * tq:(i + 1) * tq].T.astype(BF16)
            o_ref[0, :, h * MLA_V:(h + 1) * MLA_V] = _dot(o_lat, wuv_ref[h]).astype(o_ref.dtype)
        yield

    _interleave([group(g) for g in range(MLA_HEADS // MLA_GROUP)])


def _mla_attention(q, k, wuv, tq):
    bsz, _, seq, _ = q.shape
    return pl.pallas_call(
        functools.partial(_mla_kernel, tq=tq),
        grid=(bsz, seq // tq),
        in_specs=[pl.BlockSpec((1, MLA_HEADS, tq, MLA_QK_WIDTH), lambda b, j: (b, 0, j, 0)),
                  pl.BlockSpec((1, seq, k.shape[-1]), lambda b, j: (b, 0, 0)),
                  _const_spec(wuv.shape)],
        out_specs=pl.BlockSpec((1, tq, MLA_HEADS * MLA_V), lambda b, j: (b, j, 0)),
        out_shape=jax.ShapeDtypeStruct((bsz, seq, MLA_HEADS * MLA_V), BF16),
        compiler_params=_params(2),
        name="mla_attention",
    )(q, k, wuv)


def _even_out_kernel(yf_ref, yb_ref, xs_ref, z_ref, om_ref, x_ref, dskip_ref, snw_ref, wo_ref, nw_ref, o_ref):
    y = yf_ref[...].astype(F32) + yb_ref[...].astype(F32) + xs_ref[...].astype(F32) * dskip_ref[...]
    y = y * _silu(z_ref[...].astype(F32))
    snw = snw_ref[...]
    gw = SSD_GROUP_WIDTH
    m = _dot(om_ref[...], wo_ref[SSD_INNER:, :])
    for g in range(SSD_GROUPS):
        yg = _rms(y[:, g * gw:(g + 1) * gw], snw[:, g * gw:(g + 1) * gw]).astype(BF16)
        m = m + _dot(yg, wo_ref[g * gw:(g + 1) * gw, :])
    o_ref[...] = x_ref[...] + _rms(m, nw_ref[...])


def _even_out(yf, yb, xs, z, om, x, dskip, snw, wo, nw, tm):
    t = x.shape[0]
    tok = pl.BlockSpec((tm, D_MODEL), lambda i: (i, 0))
    return pl.pallas_call(
        _even_out_kernel,
        grid=(t // tm,),
        in_specs=[tok, tok, tok, tok, tok, tok, _const_spec(dskip.shape), _const_spec(snw.shape),
                  _const_spec(wo.shape), _const_spec(nw.shape)],
        out_specs=tok,
        out_shape=jax.ShapeDtypeStruct(x.shape, F32),
        compiler_params=_params(1),
        name="even_out",
    )(yf, yb, xs, z, om, x, dskip, snw, wo, nw)


def _gla_in_kernel(x_ref, nw_ref, w_ref, wgk_ref, bgk_ref, q_ref, k_ref, v_ref, g_ref, gk_ref):
    h = _rms(x_ref[...], nw_ref[...]).astype(BF16)
    o1, o2, o3, o4 = GLA_KEY, 2 * GLA_KEY, 2 * GLA_KEY + GLA_VAL, 2 * GLA_KEY + 2 * GLA_VAL

    def gates():
        lr = _dot(h, w_ref[:, o4:]).astype(BF16)
        yield
        pre = _dot(lr, wgk_ref[...]) + bgk_ref[...]
        yield
        gk_ref[...] = (jnp.minimum(pre, 0.0) - jnp.log1p(jnp.exp(-jnp.abs(pre)))) / GLA_GATE_NORM
        yield

    def projections():
        q_ref[...] = (_dot(h, w_ref[:, 0:o1]) * GLA_HK ** -0.5).astype(BF16)
        yield
        k_ref[...] = _dot(h, w_ref[:, o1:o2]).astype(BF16)
        yield
        v_ref[...] = _dot(h, w_ref[:, o2:o3]).astype(BF16)
        yield
        g_ref[...] = _dot(h, w_ref[:, o3:o4]).astype(BF16)
        yield

    _interleave([gates(), projections()])


def _gla_in(x, nw, w, wgk, bgk, tm):
    t = x.shape[0]

    def tok(width):
        return pl.BlockSpec((tm, width), lambda i: (i, 0))

    def out(width, dtype):
        return jax.ShapeDtypeStruct((t, width), dtype)

    return pl.pallas_call(
        _gla_in_kernel,
        grid=(t // tm,),
        in_specs=[tok(D_MODEL), _const_spec(nw.shape), _const_spec(w.shape), _const_spec(wgk.shape),
                  _const_spec(bgk.shape)],
        out_specs=[tok(GLA_KEY), tok(GLA_KEY), tok(GLA_VAL), tok(GLA_VAL), tok(2 * GLA_KEY)],
        out_shape=[out(GLA_KEY, BF16), out(GLA_KEY, BF16), out(GLA_VAL, BF16), out(GLA_VAL, BF16),
                   out(2 * GLA_KEY, F32)],
        compiler_params=_params(1),
        name="gla_in",
    )(x, nw, w, wgk, bgk)


def _gla_stream(q_ref, k_ref, v_ref, g_ref, o_ref, st_ref, reverse, nck):
    L = GLA_CHUNK
    row = lax.broadcasted_iota(jnp.int32, (L, L), 0)
    col = lax.broadcasted_iota(jnp.int32, (L, L), 1)
    tri = (row <= col) if reverse else (row >= col)
    keys = [slice(h * GLA_HK, (h + 1) * GLA_HK) for h in range(GLA_HEADS)]
    vals = [slice(h * GLA_HV, (h + 1) * GLA_HV) for h in range(GLA_HEADS)]
    staged = []
    for ci in (range(nck - 1, -1, -1) if reverse else range(nck)):
        rows = pl.ds(ci * L, L)
        cum = _mask_dot(tri, g_ref[rows, :])
        yield
        tot = cum[0:1] if reverse else cum[L - 1:L]
        qf = q_ref[rows, :].astype(F32)
        kf = k_ref[rows, :].astype(F32)
        qt = (qf * jnp.exp(cum)).astype(BF16)
        kt = (kf * jnp.exp(-cum)).astype(BF16)
        ke = (kf * jnp.exp(tot - cum)).astype(BF16)
        yield
        att = [_dot_nt(qt[:, ks], kt[:, ks]) for ks in keys]
        yield
        v = v_ref[rows, :]
        intra = [_dot(jnp.where(tri, att[h], 0.0).astype(BF16), v[:, vals[h]]) for h in range(GLA_HEADS)]
        upd = [_dot_tn(v[:, vals[h]], ke[:, keys[h]]) for h in range(GLA_HEADS)]
        yield
        staged.append((rows, qt, jnp.exp(tot), intra, jnp.concatenate(upd, axis=1)))
    for rows, qt, etot, intra, upd in staged:
        st = st_ref[...]
        stb = st.astype(BF16)
        outs = [intra[h] + _dot_nt(qt[:, keys[h]], stb[:, keys[h]]) for h in range(GLA_HEADS)]
        st_ref[...] = st * etot + upd
        yield
        o_ref[rows, :] = jnp.concatenate(outs, axis=1).astype(o_ref.dtype)
        yield


def _gla_kernel(qf_ref, qb_ref, kf_ref, kb_ref, vf_ref, vb_ref, gf_ref, gb_ref, of_ref, ob_ref, stf_ref, stb_ref,
                *, nck, nseq):
    @pl.when(pl.program_id(1) == 0)
    def _():
        stf_ref[...] = jnp.zeros_like(stf_ref)
        stb_ref[...] = jnp.zeros_like(stb_ref)

    streams = []
    for s in range(nseq):
        streams.append(_gla_stream(qf_ref.at[s], kf_ref.at[s], vf_ref.at[s], gf_ref.at[s], of_ref.at[s],
                                   stf_ref.at[s], False, nck))
        streams.append(_gla_stream(qb_ref.at[s], kb_ref.at[s], vb_ref.at[s], gb_ref.at[s], ob_ref.at[s],
                                   stb_ref.at[s], True, nck))
    _interleave(streams)


def _gla_scan(q, k, v, gk, tb, nseq):
    bsz, seq, _ = q.shape
    nb = seq // tb

    def fwd(width, cb=0):
        return pl.BlockSpec((nseq, tb, width), lambda b, j: (b, j, cb))

    def bwd(width, cb=0):
        return pl.BlockSpec((nseq, tb, width), lambda b, j: (b, nb - 1 - j, cb))

    out = jax.ShapeDtypeStruct((bsz, seq, GLA_VAL), BF16)
    state = pltpu.VMEM((nseq, GLA_HV, GLA_KEY), F32)
    return pl.pallas_call(
        functools.partial(_gla_kernel, nck=tb // GLA_CHUNK, nseq=nseq),
        grid=(bsz // nseq, nb),
        in_specs=[fwd(GLA_KEY), bwd(GLA_KEY), fwd(GLA_KEY), bwd(GLA_KEY), fwd(GLA_VAL), bwd(GLA_VAL),
                  fwd(GLA_KEY, 0), bwd(GLA_KEY, 1)],
        out_specs=[fwd(GLA_VAL), bwd(GLA_VAL)],
        out_shape=[out, out],
        scratch_shapes=[state, state],
        compiler_params=_params(2),
        name="gla_scan",
    )(q, q, k, k, v, v, gk, gk)


def _gla_out_kernel(of_ref, ob_ref, g_ref, x_ref, gnw_ref, wo_ref, nw_ref, o_ref):
    o = of_ref[...].astype(F32) + ob_ref[...].astype(F32)
    gate = _silu(g_ref[...].astype(F32))
    gnw = gnw_ref[...]
    m = None
    for h in range(GLA_HEADS):
        hs = slice(h * GLA_HV, (h + 1) * GLA_HV)
        oh = (_rms(o[:, hs], gnw) * gate[:, hs]).astype(BF16)
        d = _dot(oh, wo_ref[hs, :])
        m = d if m is None else m + d
    o_ref[...] = x_ref[...] + _rms(m, nw_ref[...])


def _gla_out(of, ob, g, x, gnw, wo, nw, tm):
    t = x.shape[0]
    tok = pl.BlockSpec((tm, D_MODEL), lambda i: (i, 0))
    return pl.pallas_call(
        _gla_out_kernel,
        grid=(t // tm,),
        in_specs=[tok, tok, tok, tok, _const_spec(gnw.shape), _const_spec(wo.shape), _const_spec(nw.shape)],
        out_specs=tok,
        out_shape=jax.ShapeDtypeStruct(x.shape, F32),
        compiler_params=_params(1),
        name="gla_out",
    )(of, ob, g, x, gnw, wo, nw)


def _ffn_kernel(xm_ref, xp_ref, xn_ref, nw_ref, wg_ref, wu_ref, cwg_ref, cwu_ref, wo_ref, nwo_ref, o_ref,
                hcat_ref, *, tm, nt, nchunk):
    j = pl.program_id(1)
    _fill_halo_rows(hcat_ref, xm_ref, xp_ref, xn_ref, nw_ref[...], j, nt, tm)
    hcat = hcat_ref[...]

    def project(c):
        return _dot(hcat, wg_ref[c]), _dot(hcat, wu_ref[c])

    u_next = project(0)
    acc = None
    acts = []
    for c in range(nchunk):
        u_gate, u_up = u_next
        if c + 1 < nchunk:
            u_next = project(c + 1)
        gate = _conv3(u_gate, cwg_ref[c], tm)
        up = _conv3(u_up, cwu_ref[c], tm)
        acts.append((_gelu_tanh(gate) * up).astype(BF16))
        if len(acts) == 2 or c + 1 == nchunk:
            c0 = c + 1 - len(acts)
            act = acts[0] if len(acts) == 1 else jnp.concatenate(acts, axis=1)
            d = _dot(act, wo_ref[c0 * FFN_COL_CHUNK:(c + 1) * FFN_COL_CHUNK, :])
            acc = d if acc is None else acc + d
            acts = []
    o_ref[0] = xm_ref[0] + _rms(acc, nwo_ref[...])


def _ffn(x, nw, wg, wu, cwg, cwu, wo, nwo, tm):
    bsz, seq, _ = x.shape
    nt = seq // tm
    main, prev, nxt = _halo_specs(tm, seq)
    return pl.pallas_call(
        functools.partial(_ffn_kernel, tm=tm, nt=nt, nchunk=wg.shape[0]),
        grid=(bsz, nt),
        in_specs=[main, prev, nxt, _const_spec(nw.shape), _const_spec(wg.shape), _const_spec(wu.shape),
                  _const_spec(cwg.shape), _const_spec(cwu.shape), _const_spec(wo.shape), _const_spec(nwo.shape)],
        out_specs=pl.BlockSpec((1, tm, D_MODEL), lambda b, j: (b, j, 0)),
        out_shape=jax.ShapeDtypeStruct(x.shape, F32),
        scratch_shapes=[pltpu.VMEM((tm + HALO, D_MODEL), BF16)],
        compiler_params=_params(2),
        name="conv_ffn",
    )(x, x, x, nw, wg, wu, cwg, cwu, wo, nwo)


def _pad_cols(w, width):
    return jnp.pad(w, ((0, 0),) * (w.ndim - 1) + ((0, width - w.shape[-1]),))


def _swap_halves(w):
    half = w.shape[-1] // 2
    return jnp.concatenate([w[..., half:], w[..., :half]], axis=-1)


def _row(v):
    return v.reshape(1, -1).astype(F32)


def _prep_even(w_in, conv_w, conv_b, a_log, dt_bias, d_skip, ssd_norm_w, q_norm_w, w_qb, kv_norm_w, w_kvb, w_out):
    o1 = SSD_INNER
    o2 = o1 + SSD_CONV_DIM
    o3 = o2 + 2 * SSD_HEADS
    o4 = o3 + MLA_Q_LORA
    o5 = o4 + MLA_KV_LORA
    w_kr = w_in[:, o5:]
    wall = jnp.concatenate([w_in[:, :o2], w_in[:, o3:o5], _pad_cols(w_kr, LANES), _pad_cols(_swap_halves(w_kr), LANES),
                            _pad_cols(w_in[:, o2:o3], LANES)], axis=1).astype(BF16)
    cw = jnp.concatenate([conv_w, conv_b[None, :]], axis=0).astype(F32)
    dtb = _pad_cols(_row(dt_bias), LANES)
    alog = _pad_cols(_row(a_log), LANES)
    dskip = _row(jnp.repeat(d_skip, SSD_HEAD_DIM))
    wq3 = w_qb.reshape(MLA_Q_LORA, MLA_HEADS, MLA_NOPE + MLA_ROPE)
    rope = wq3[..., MLA_NOPE:]
    wq = jnp.concatenate([wq3[..., :MLA_NOPE].reshape(MLA_Q_LORA, -1),
                          _pad_cols(rope, LANES).reshape(MLA_Q_LORA, -1),
                          _pad_cols(_swap_halves(rope), LANES).reshape(MLA_Q_LORA, -1)], axis=1).astype(BF16)
    wkv3 = w_kvb.reshape(MLA_KV_LORA, MLA_HEADS, MLA_NOPE + MLA_V)
    wuk = jnp.transpose(wkv3[..., :MLA_NOPE], (1, 2, 0)).astype(BF16)
    wuv = jnp.transpose(wkv3[..., MLA_NOPE:], (1, 0, 2)).astype(BF16)
    return dict(wall=wall, cw=cw, dtb=dtb, alog=alog, dskip=dskip, snw=_row(ssd_norm_w), qnw=_row(q_norm_w), wq=wq,
                wuk=wuk, kvnw=_row(kv_norm_w), wuv=wuv, wo=w_out.astype(BF16))


def _rope_tables(seq):
    inv = 1.0 / (ROPE_THETA ** (jnp.arange(0, MLA_ROPE, 2, dtype=F32) / MLA_ROPE))
    ang = jnp.arange(seq, dtype=F32)[:, None] * inv[None, :]
    cos, sin = jnp.cos(ang), jnp.sin(ang)
    return (_pad_cols(jnp.concatenate([cos, cos], axis=1), LANES),
            _pad_cols(jnp.concatenate([-sin, sin], axis=1), LANES))


def _prep_gla(w_in, w_gk2, b_gk, norm_w, w_out):
    w = _pad_cols(w_in, 2 * GLA_KEY + 2 * GLA_VAL + LANES).astype(BF16)
    wgk = jnp.zeros((LANES, 2 * GLA_KEY), F32)
    for d in range(2):
        wgk = wgk.at[d * GLA_GATE_RANK:(d + 1) * GLA_GATE_RANK, d * GLA_KEY:(d + 1) * GLA_KEY].set(w_gk2[d])
    return dict(w=w, wgk=wgk.astype(BF16), bgk=_row(b_gk), gnw=_row(norm_w), wo=w_out.astype(BF16))


def _prep_ffn(w_in, conv_w, conv_b, w_out):
    nchunk = D_FF // FFN_COL_CHUNK

    def cols(w):
        return jnp.transpose(w.reshape(w.shape[0], nchunk, FFN_COL_CHUNK), (1, 0, 2))

    cw = jnp.concatenate([conv_w, conv_b[None, :], jnp.zeros((4, 2 * D_FF), F32)], axis=0).astype(F32)
    return dict(wg=cols(w_in[:, :D_FF]).astype(BF16), wu=cols(w_in[:, D_FF:]).astype(BF16),
                cwg=cols(cw[:, :D_FF]), cwu=cols(cw[:, D_FF:]),
                wo=w_out.astype(BF16))


def _tile(total, pref):
    return min(total, pref)


def _prepare(p):
    layers = []
    for layer in range(p['norm_w'].shape[0]):
        i = layer // 2
        if layer % 2 == 0:
            mixer = _prep_even(p['hyb_w_in'][i], p['ssd_conv_w'][i], p['ssd_conv_b'][i], p['ssd_a_log'][i],
                               p['ssd_dt_bias'][i], p['ssd_d'][i], p['ssd_norm_w'][i], p['mla_q_norm_w'][i],
                               p['mla_w_qb'][i], p['mla_kv_norm_w'][i], p['mla_w_kvb'][i], p['hyb_w_out'][i])
        else:
            mixer = _prep_gla(p['gla_w_in'][i], p['gla_w_gk2'][i], p['gla_b_gk'][i], p['gla_norm_w'][i],
                              p['gla_w_out'][i])
        ffn = _prep_ffn(p['ffn_w_in'][layer], p['ffn_conv_w'][layer], p['ffn_conv_b'][layer], p['ffn_w_out'][layer])
        layers.append((p['norm_w'][layer].astype(F32), mixer, ffn))
    return layers


def _trunk(x, layers, tiles=None):
    tiles = dict(dict(even_in=512, ssd=512, ssd_seqs=1, mla=256, tok=512, gla=256, gla_seqs=2, ffn=512), **(tiles or {}))
    bsz, seq, d = x.shape
    cos, sin = _rope_tables(seq)
    flat = lambda t: t.reshape(bsz * seq, t.shape[-1])
    for layer, (nw, e, f) in enumerate(layers):
        if layer % 2 == 0:
            z, xs, bc, dt, q, k = _even_in(x, nw[0:1], e['wall'], e['cw'], e['dtb'], e['qnw'], e['wq'], e['wuk'],
                                           e['kvnw'], cos, sin, _tile(seq, tiles['even_in']))
            yf, yb = _ssd_scan(xs, bc, dt, e['alog'], _tile(seq, tiles['ssd']), math.gcd(bsz, tiles['ssd_seqs']))
            om = _mla_attention(q, k, e['wuv'], _tile(seq, tiles['mla']))
            x = _even_out(flat(yf), flat(yb), flat(xs), flat(z), flat(om), flat(x), e['dskip'], e['snw'], e['wo'],
                          nw[1:2], _tile(bsz * seq, tiles['tok'])).reshape(bsz, seq, d)
        else:
            q, k, v, g, gk = _gla_in(flat(x), nw[0:1], e['w'], e['wgk'], e['bgk'], _tile(bsz * seq, tiles['tok']))
            r3 = lambda t: t.reshape(bsz, seq, t.shape[-1])
            of, ob = _gla_scan(r3(q), r3(k), r3(v), r3(gk), _tile(seq, tiles['gla']), math.gcd(bsz, tiles['gla_seqs']))
            x = _gla_out(flat(of), flat(ob), g, flat(x), e['gnw'], e['wo'], nw[1:2],
                         _tile(bsz * seq, tiles['tok'])).reshape(bsz, seq, d)
        x = _ffn(x, nw[2:3], f['wg'], f['wu'], f['cwg'], f['cwu'], f['wo'], nw[3:4], _tile(seq, tiles['ffn']))
    return x


def kernel(x_prompt, x_sample, hyb_w_in, ssd_conv_w, ssd_conv_b, ssd_a_log, ssd_dt_bias, ssd_d, ssd_norm_w, mla_q_norm_w, mla_w_qb, mla_kv_norm_w, mla_w_kvb, hyb_w_out, gla_w_in, gla_w_gk2, gla_b_gk, gla_norm_w, gla_w_out, ffn_w_in, ffn_conv_w, ffn_conv_b, ffn_w_out, norm_w):
    params = dict(hyb_w_in=hyb_w_in, ssd_conv_w=ssd_conv_w, ssd_conv_b=ssd_conv_b, ssd_a_log=ssd_a_log,
                  ssd_dt_bias=ssd_dt_bias, ssd_d=ssd_d, ssd_norm_w=ssd_norm_w,
                  mla_q_norm_w=mla_q_norm_w, mla_w_qb=mla_w_qb, mla_kv_norm_w=mla_kv_norm_w,
                  mla_w_kvb=mla_w_kvb, hyb_w_out=hyb_w_out,
                  gla_w_in=gla_w_in, gla_w_gk2=gla_w_gk2, gla_b_gk=gla_b_gk, gla_norm_w=gla_norm_w,
                  gla_w_out=gla_w_out, ffn_w_in=ffn_w_in, ffn_conv_w=ffn_conv_w, ffn_conv_b=ffn_conv_b,
                  ffn_w_out=ffn_w_out, norm_w=norm_w)
    layers = _prepare(params)
    return (_trunk(x_prompt, layers), _trunk(x_sample, layers))
```

```python
import functools
import math

import jax
import jax.numpy as jnp
from jax import lax
from jax.experimental import pallas as pl
from jax.experimental.pallas import tpu as pltpu

F32 = jnp.float32
BF16 = jnp.bfloat16
LOG2_E = math.log2(math.e)

D_MODEL = 1024
EPS = 1e-6

SSD_HEADS = 16
SSD_HEAD_DIM = 64
SSD_INNER = SSD_HEADS * SSD_HEAD_DIM
SSD_GROUPS = 2
SSD_HPG = SSD_HEADS // SSD_GROUPS
SSD_STATE = 128
SSD_CHUNK = 128
SSD_CONV_DIM = SSD_INNER + 2 * SSD_GROUPS * SSD_STATE
SSD_GROUP_WIDTH = SSD_INNER // SSD_GROUPS

MLA_HEADS = 8
MLA_Q_LORA = 256
MLA_KV_LORA = 128
MLA_NOPE = 128
MLA_ROPE = 64
MLA_V = 128
ROPE_THETA = 10000.0
MLA_QK_WIDTH = 256
MLA_GROUP = 4

GLA_HEADS = 4
GLA_KEY = D_MODEL // 2
GLA_VAL = D_MODEL
GLA_HK = GLA_KEY // GLA_HEADS
GLA_HV = GLA_VAL // GLA_HEADS
GLA_GATE_RANK = 16
GLA_GATE_NORM = 16.0
GLA_CHUNK = 64

D_FF = 2816
FFN_COL_CHUNK = 256

LANES = 128
SUBLANES = 8
HALO = 2 * SUBLANES
VMEM_LIMIT_BYTES = 56 * 1024 * 1024

_E_Z = 0
_E_XBC = _E_Z + SSD_INNER
_E_QA = _E_XBC + SSD_CONV_DIM
_E_CKV = _E_QA + MLA_Q_LORA
_E_KR = _E_CKV + MLA_KV_LORA
_E_KRSW = _E_KR + LANES
_E_DT = _E_KRSW + LANES
_E_END = _E_DT + LANES


def _dot(a, b):
    return jnp.dot(a, b, preferred_element_type=F32)


def _dot_nt(a, b):
    return lax.dot_general(a, b, (((1,), (1,)), ((), ())), preferred_element_type=F32)


def _dot_tn(a, b):
    return lax.dot_general(a, b, (((0,), (0,)), ((), ())), preferred_element_type=F32)


def _mask_dot(mask, v):
    t = mask.astype(BF16)
    h1 = v.astype(BF16)
    r1 = v - h1.astype(F32)
    h2 = r1.astype(BF16)
    h3 = (r1 - h2.astype(F32)).astype(BF16)
    if mask.shape[1] % LANES == 0:
        return _dot(jnp.concatenate([t, t, t], axis=1), jnp.concatenate([h1, h2, h3], axis=0))
    return _dot(t, h1) + _dot(t, h2) + _dot(t, h3)


def _rms(x, w):
    return x * lax.rsqrt(jnp.mean(x * x, axis=-1, keepdims=True) + EPS) * w


def _softplus(x):
    return jnp.maximum(x, 0.0) + jnp.log1p(jnp.exp(-jnp.abs(x)))


def _silu(x):
    return x * jax.nn.sigmoid(x)


def _gelu_tanh(x):
    k = -2.0 * math.sqrt(2.0 / math.pi) * LOG2_E
    return x / (1.0 + jnp.exp2((x * x * (k * 0.044715) + k) * x))


def _params(n_axes, flags=None):
    return pltpu.CompilerParams(dimension_semantics=("arbitrary",) * n_axes,
                                vmem_limit_bytes=VMEM_LIMIT_BYTES, flags=flags)


def _interleave(streams, skew=0):
    pending = list(enumerate(streams))
    rnd = 0
    while pending:
        alive = []
        for i, stream in pending:
            if rnd >= skew * i:
                try:
                    next(stream)
                except StopIteration:
                    continue
            alive.append((i, stream))
        pending = alive
        rnd += 1


def _const_spec(shape):
    nd = len(shape)
    return pl.BlockSpec(shape, lambda *_: (0,) * nd, pipeline_mode=pl.Buffered(1))


def _halo_specs(tm, seq):
    per = tm // SUBLANES
    last = seq // SUBLANES - 1
    main = pl.BlockSpec((1, tm, D_MODEL), lambda b, j: (b, j, 0))
    prev = pl.BlockSpec((1, SUBLANES, D_MODEL), lambda b, j: (b, jnp.maximum(j * per - 1, 0), 0))
    nxt = pl.BlockSpec((1, SUBLANES, D_MODEL), lambda b, j: (b, jnp.minimum((j + 1) * per, last), 0))
    return main, prev, nxt


def _fill_halo_rows(hcat_ref, xm_ref, xp_ref, xn_ref, nw, j, nt, tm):
    hn = jnp.where(j == nt - 1, 0.0, _rms(xn_ref[0], nw))
    hp = jnp.where(j == 0, 0.0, _rms(xp_ref[0], nw))
    hcat_ref[0:HALO, :] = jnp.concatenate([hn, hp], axis=0).astype(BF16)
    hcat_ref[HALO:, :] = _rms(xm_ref[0], nw).astype(BF16)


def _conv3(u, cw, tm):
    rows = tm + HALO
    um1 = pltpu.roll(u, 1, 0)[HALO:]
    up1 = pltpu.roll(u, rows - 1, 0)[HALO:]
    return um1 * cw[0:1] + u[HALO:] * cw[1:2] + up1 * cw[2:3] + cw[3:4]


def _even_in_kernel(xm_ref, xp_ref, xn_ref, nw_ref, wall_ref, cw_ref, dtb_ref, qnw_ref, wq_ref, wuk_ref,
                    kvnw_ref, cos_ref, sin_ref,
                    z_ref, xs_ref, bc_ref, dt_ref, q_ref, k_ref, hcat_ref, *, tm, nt):
    j = pl.program_id(1)
    _fill_halo_rows(hcat_ref, xm_ref, xp_ref, xn_ref, nw_ref[...], j, nt, tm)
    hcat = hcat_ref[...]
    hm = hcat_ref[HALO:, :]

    def ssd_columns():
        for c0 in range(0, SSD_CONV_DIM, 512):
            u = _dot(hcat, wall_ref[:, _E_XBC + c0:_E_XBC + c0 + 512])
            yield
            y = _silu(_conv3(u, cw_ref[:, c0:c0 + 512], tm)).astype(BF16)
            if c0 < SSD_INNER:
                xs_ref[0, :, c0:c0 + 512] = y
            else:
                bc_ref[0] = y
            yield

    def gate_columns():
        half = SSD_INNER // 2
        for c0 in (0, half):
            z = _dot(hm, wall_ref[:, _E_Z + c0:_E_Z + c0 + half])
            yield
            z_ref[0, :, c0:c0 + half] = z.astype(BF16)
            yield

    def mla_columns():
        rest = _dot(hm, wall_ref[:, _E_QA:_E_END])
        yield
        o = -_E_QA
        qa = rest[:, o + _E_QA:o + _E_CKV]
        ckv = rest[:, o + _E_CKV:o + _E_KR]
        kr = rest[:, o + _E_KR:o + _E_KRSW]
        krsw = rest[:, o + _E_KRSW:o + _E_DT]
        dtr = rest[:, o + _E_DT:o + _E_END]
        cos = cos_ref[...]
        sin = sin_ref[...]
        scale = (MLA_NOPE + MLA_ROPE) ** -0.5 * LOG2_E
        hq = _rms(qa, qnw_ref[...]).astype(BF16)
        nh = MLA_HEADS * LANES
        qall = _dot(hq, wq_ref[...])
        yield
        dt_ref[0] = _softplus(dtr + dtb_ref[...])
        k_ref[0, :, 0:LANES] = _rms(ckv, kvnw_ref[...]).astype(BF16)
        k_ref[0, :, LANES:2 * LANES] = (kr * cos + krsw * sin).astype(BF16)
        yield
        for h in range(MLA_HEADS):
            qn = qall[:, h * LANES:(h + 1) * LANES].astype(BF16)
            ql = _dot(qn, wuk_ref[h])
            qr = (qall[:, nh + h * LANES:nh + (h + 1) * LANES] * cos
                  + qall[:, 2 * nh + h * LANES:2 * nh + (h + 1) * LANES] * sin)
            q_ref[0, h, :, 0:LANES] = (ql * scale).astype(BF16)
            q_ref[0, h, :, LANES:MLA_QK_WIDTH] = (qr * scale).astype(BF16)
            if h % 2 == 1:
                yield

    _interleave([mla_columns(), ssd_columns(), gate_columns()])


def _even_in(x, nw, wall, cw, dtb, qnw, wq, wuk, kvnw, cos, sin, tm):
    bsz, seq, _ = x.shape
    nt = seq // tm
    main, prev, nxt = _halo_specs(tm, seq)

    def tok(width):
        return pl.BlockSpec((1, tm, width), lambda b, j: (b, j, 0))

    def out(width, dtype):
        return jax.ShapeDtypeStruct((bsz, seq, width), dtype)

    rope_spec = pl.BlockSpec((tm, LANES), lambda b, j: (j, 0))
    return pl.pallas_call(
        functools.partial(_even_in_kernel, tm=tm, nt=nt),
        grid=(bsz, nt),
        in_specs=[main, prev, nxt, _const_spec(nw.shape), _const_spec(wall.shape), _const_spec(cw.shape),
                  _const_spec(dtb.shape), _const_spec(qnw.shape), _const_spec(wq.shape), _const_spec(wuk.shape),
                  _const_spec(kvnw.shape), rope_spec, rope_spec],
        out_specs=[tok(SSD_INNER), tok(SSD_INNER), tok(2 * SSD_GROUPS * SSD_STATE), tok(LANES),
                   pl.BlockSpec((1, MLA_HEADS, tm, MLA_QK_WIDTH), lambda b, j: (b, 0, j, 0)), tok(MLA_QK_WIDTH)],
        out_shape=[out(SSD_INNER, BF16), out(SSD_INNER, BF16), out(2 * SSD_GROUPS * SSD_STATE, BF16),
                   out(LANES, F32), jax.ShapeDtypeStruct((bsz, MLA_HEADS, seq, MLA_QK_WIDTH), BF16),
                   out(MLA_QK_WIDTH, BF16)],
        scratch_shapes=[pltpu.VMEM((tm + HALO, D_MODEL), BF16)],
        compiler_params=_params(2),
        name="even_in",
    )(x, x, x, nw, wall, cw, dtb, qnw, wq, wuk, kvnw, cos, sin)


def _expand_heads(v, expand2):
    hi = v.astype(BF16)
    lo = (v - hi.astype(F32)).astype(BF16)
    return _dot(jnp.concatenate([hi, lo], axis=1), expand2)


def _ssd_stream(x_ref, bc_ref, dt_ref, y_ref, st_ref, a, reverse, lane0, nck):
    L = SSD_CHUNK
    row = lax.broadcasted_iota(jnp.int32, (L, L), 0)
    col = lax.broadcasted_iota(jnp.int32, (L, L), 1)
    tri = (row <= col) if reverse else (row >= col)
    er = lax.broadcasted_iota(jnp.int32, (LANES, SSD_INNER), 0)
    ec = lax.broadcasted_iota(jnp.int32, (LANES, SSD_INNER), 1)
    head_of_lane = lax.shift_right_logical(ec, int(math.log2(SSD_HEAD_DIM)))
    expand = jnp.where(er - lane0 == head_of_lane, 1.0, 0.0).astype(BF16)
    expand = jnp.concatenate([expand, expand], axis=0)
    lane = lax.broadcasted_iota(jnp.int32, (L, LANES), 1)
    low_half = lane < SSD_HEAD_DIM
    gw = SSD_GROUP_WIDTH
    groups = [slice(g * gw, (g + 1) * gw) for g in range(SSD_GROUPS)]
    staged = []
    for ci in (range(nck - 1, -1, -1) if reverse else range(nck)):
        rows = pl.ds(ci * L, L)
        dt = dt_ref[rows, :]
        cum = _mask_dot(tri, dt * a)
        yield
        shifted_t = (cum - jnp.log(dt)).T
        tot = cum[0:1] if reverse else cum[L - 1:L]
        ecum = _expand_heads(jnp.exp(cum), expand)
        wend = _expand_heads(jnp.exp(tot - cum) * dt, expand)
        etot = _expand_heads(jnp.broadcast_to(jnp.exp(tot), (SUBLANES, LANES)), expand)[0:1]
        yield
        x = x_ref[rows, :]
        bc = bc_ref[rows, :]
        xw = (x.astype(F32) * wend).astype(BF16)
        bs = [bc[:, g * SSD_STATE:(g + 1) * SSD_STATE] for g in range(SSD_GROUPS)]
        cs = [bc[:, (SSD_GROUPS + g) * SSD_STATE:(SSD_GROUPS + g + 1) * SSD_STATE] for g in range(SSD_GROUPS)]
        cb = [_dot_nt(cs[g], bs[g]) for g in range(SSD_GROUPS)]
        upd = jnp.concatenate([_dot_tn(bs[g], xw[:, groups[g]]) for g in range(SSD_GROUPS)], axis=1)
        yield
        intra = []
        for g in range(SSD_GROUPS):
            for p in range(SSD_HPG // 2):
                lo = g * gw + p * LANES
                x_pair = x[:, lo:lo + LANES]
                zero = jnp.zeros_like(x_pair)
                x_stack = jnp.concatenate([jnp.where(low_half, x_pair, zero), jnp.where(low_half, zero, x_pair)],
                                          axis=0)
                pair = []
                for e in (2 * p, 2 * p + 1):
                    c = lane0 + g * SSD_HPG + e
                    diff = cum[:, c:c + 1] - shifted_t[c:c + 1, :]
                    pair.append((cb[g] * jnp.exp(jnp.where(tri, diff, -jnp.inf))).astype(BF16))
                intra.append(_dot(jnp.concatenate(pair, axis=1), x_stack))
            yield
        staged.append((rows, cs, ecum, etot, intra, upd))
    for rows, cs, ecum, etot, intra, upd in staged:
        st = st_ref[...]
        stb = st.astype(BF16)
        ys = []
        for g in range(SSD_GROUPS):
            y_inter = _dot(cs[g], stb[:, groups[g]])
            for p in range(SSD_HPG // 2):
                lo = g * gw + p * LANES
                ys.append(intra[g * (SSD_HPG // 2) + p] + y_inter[:, p * LANES:(p + 1) * LANES] * ecum[:, lo:lo + LANES])
        st_ref[...] = st * etot + upd
        yield
        y_ref[rows, :] = jnp.concatenate(ys, axis=1).astype(y_ref.dtype)
        yield


def _ssd_kernel(xf_ref, xb_ref, bcf_ref, bcb_ref, dtf_ref, dtb_ref, alog_ref, yf_ref, yb_ref, stf_ref, stb_ref,
                *, nck, nseq):
    @pl.when(pl.program_id(1) == 0)
    def _():
        stf_ref[...] = jnp.zeros_like(stf_ref)
        stb_ref[...] = jnp.zeros_like(stb_ref)

    a = -jnp.exp(alog_ref[...])
    streams = []
    for s in range(nseq):
        streams.append(_ssd_stream(xf_ref.at[s], bcf_ref.at[s], dtf_ref.at[s], yf_ref.at[s], stf_ref.at[s], a,
                                   False, 0, nck))
        streams.append(_ssd_stream(xb_ref.at[s], bcb_ref.at[s], dtb_ref.at[s], yb_ref.at[s], stb_ref.at[s], a,
                                   True, SSD_HEADS, nck))
    _interleave(streams)


def _ssd_scan(xs, bc, dt, alog, tb, nseq):
    bsz, seq, _ = xs.shape
    nb = seq // tb

    def fwd(width):
        return pl.BlockSpec((nseq, tb, width), lambda b, j: (b, j, 0))

    def bwd(width):
        return pl.BlockSpec((nseq, tb, width), lambda b, j: (b, nb - 1 - j, 0))

    out = jax.ShapeDtypeStruct((bsz, seq, SSD_INNER), BF16)
    state = pltpu.VMEM((nseq, SSD_STATE, SSD_INNER), F32)
    return pl.pallas_call(
        functools.partial(_ssd_kernel, nck=tb // SSD_CHUNK, nseq=nseq),
        grid=(bsz // nseq, nb),
        in_specs=[fwd(SSD_INNER), bwd(SSD_INNER), fwd(bc.shape[-1]), bwd(bc.shape[-1]), fwd(LANES), bwd(LANES),
                  _const_spec(alog.shape)],
        out_specs=[fwd(SSD_INNER), bwd(SSD_INNER)],
        out_shape=[out, out],
        scratch_shapes=[state, state],
        compiler_params=_params(2),
        name="ssd_scan",
    )(xs, xs, bc, bc, dt, dt, alog)


def _mla_kernel(q_ref, k_ref, wuv_ref, o_ref, *, tq):
    k = k_ref[0]
    v_t = jnp.concatenate([k[:, 0:MLA_KV_LORA].astype(F32).T.astype(BF16), jnp.ones((HALO, k.shape[0]), BF16)], axis=0)

    def group(g):
        q = q_ref[0, g * MLA_GROUP:(g + 1) * MLA_GROUP].reshape(MLA_GROUP * tq, MLA_QK_WIDTH)
        s = _dot_nt(q, k)
        yield
        m = jnp.max(s, axis=-1, keepdims=True)
        p = jnp.exp2(s - m).astype(BF16)
        o_t = _dot_nt(v_t, p)
        yield
        o_lat_t = o_t[0:MLA_KV_LORA] / o_t[MLA_KV_LORA:MLA_KV_LORA + 1]
        for i in range(MLA_GROUP):
            h = g * MLA_GROUP + i
            o_lat = o_lat_t[:, i * tq:(i + 1) * tq].T.astype(BF16)
            o_ref[0, :, h * MLA_V:(h + 1) * MLA_V] = _dot(o_lat, wuv_ref[h]).astype(o_ref.dtype)
        yield

    _interleave([group(g) for g in range(MLA_HEADS // MLA_GROUP)])


def _mla_attention(q, k, wuv, tq):
    bsz, _, seq, _ = q.shape
    return pl.pallas_call(
        functools.partial(_mla_kernel, tq=tq),
        grid=(bsz, seq // tq),
        in_specs=[pl.BlockSpec((1, MLA_HEADS, tq, MLA_QK_WIDTH), lambda b, j: (b, 0, j, 0)),
                  pl.BlockSpec((1, seq, k.shape[-1]), lambda b, j: (b, 0, 0)),
                  _const_spec(wuv.shape)],
        out_specs=pl.BlockSpec((1, tq, MLA_HEADS * MLA_V), lambda b, j: (b, j, 0)),
        out_shape=jax.ShapeDtypeStruct((bsz, seq, MLA_HEADS * MLA_V), BF16),
        compiler_params=_params(2),
        name="mla_attention",
    )(q, k, wuv)


def _even_out_kernel(yf_ref, yb_ref, xs_ref, z_ref, om_ref, x_ref, dskip_ref, snw_ref, wo_ref, nw_ref, o_ref):
    y = yf_ref[...].astype(F32) + yb_ref[...].astype(F32) + xs_ref[...].astype(F32) * dskip_ref[...]
    y = y * _silu(z_ref[...].astype(F32))
    snw = snw_ref[...]
    gw = SSD_GROUP_WIDTH
    m = _dot(om_ref[...], wo_ref[SSD_INNER:, :])
    for g in range(SSD_GROUPS):
        yg = _rms(y[:, g * gw:(g + 1) * gw], snw[:, g * gw:(g + 1) * gw]).astype(BF16)
        m = m + _dot(yg, wo_ref[g * gw:(g + 1) * gw, :])
    o_ref[...] = x_ref[...] + _rms(m, nw_ref[...])


def _even_out(yf, yb, xs, z, om, x, dskip, snw, wo, nw, tm):
    t = x.shape[0]
    tok = pl.BlockSpec((tm, D_MODEL), lambda i: (i, 0))
    return pl.pallas_call(
        _even_out_kernel,
        grid=(t // tm,),
        in_specs=[tok, tok, tok, tok, tok, tok, _const_spec(dskip.shape), _const_spec(snw.shape),
                  _const_spec(wo.shape), _const_spec(nw.shape)],
        out_specs=tok,
        out_shape=jax.ShapeDtypeStruct(x.shape, F32),
        compiler_params=_params(1),
        name="even_out",
    )(yf, yb, xs, z, om, x, dskip, snw, wo, nw)


def _gla_in_kernel(x_ref, nw_ref, w_ref, wgk_ref, bgk_ref, q_ref, k_ref, v_ref, g_ref, gk_ref):
    h = _rms(x_ref[...], nw_ref[...]).astype(BF16)
    o1, o2, o3, o4 = GLA_KEY, 2 * GLA_KEY, 2 * GLA_KEY + GLA_VAL, 2 * GLA_KEY + 2 * GLA_VAL

    def gates():
        lr = _dot(h, w_ref[:, o4:]).astype(BF16)
        yield
        pre = _dot(lr, wgk_ref[...]) + bgk_ref[...]
        yield
        gk_ref[...] = (jnp.minimum(pre, 0.0) - jnp.log1p(jnp.exp(-jnp.abs(pre)))) / GLA_GATE_NORM
        yield

    def projections():
        q_ref[...] = (_dot(h, w_ref[:, 0:o1]) * GLA_HK ** -0.5).astype(BF16)
        yield
        k_ref[...] = _dot(h, w_ref[:, o1:o2]).astype(BF16)
        yield
        v_ref[...] = _dot(h, w_ref[:, o2:o3]).astype(BF16)
        yield
        g_ref[...] = _dot(h, w_ref[:, o3:o4]).astype(BF16)
        yield

    _interleave([gates(), projections()])


def _gla_in(x, nw, w, wgk, bgk, tm):
    t = x.shape[0]

    def tok(width):
        return pl.BlockSpec((tm, width), lambda i: (i, 0))

    def out(width, dtype):
        return jax.ShapeDtypeStruct((t, width), dtype)

    return pl.pallas_call(
        _gla_in_kernel,
        grid=(t // tm,),
        in_specs=[tok(D_MODEL), _const_spec(nw.shape), _const_spec(w.shape), _const_spec(wgk.shape),
                  _const_spec(bgk.shape)],
        out_specs=[tok(GLA_KEY), tok(GLA_KEY), tok(GLA_VAL), tok(GLA_VAL), tok(2 * GLA_KEY)],
        out_shape=[out(GLA_KEY, BF16), out(GLA_KEY, BF16), out(GLA_VAL, BF16), out(GLA_VAL, BF16),
                   out(2 * GLA_KEY, F32)],
        compiler_params=_params(1),
        name="gla_in",
    )(x, nw, w, wgk, bgk)


def _gla_stream(q_ref, k_ref, v_ref, g_ref, o_ref, st_ref, reverse, nck):
    L = GLA_CHUNK
    row = lax.broadcasted_iota(jnp.int32, (L, L), 0)
    col = lax.broadcasted_iota(jnp.int32, (L, L), 1)
    tri = (row <= col) if reverse else (row >= col)
    keys = [slice(h * GLA_HK, (h + 1) * GLA_HK) for h in range(GLA_HEADS)]
    vals = [slice(h * GLA_HV, (h + 1) * GLA_HV) for h in range(GLA_HEADS)]
    staged = []
    for ci in (range(nck - 1, -1, -1) if reverse else range(nck)):
        rows = pl.ds(ci * L, L)
        cum = _mask_dot(tri, g_ref[rows, :])
        yield
        tot = cum[0:1] if reverse else cum[L - 1:L]
        qf = q_ref[rows, :].astype(F32)
        kf = k_ref[rows, :].astype(F32)
        qt = (qf * jnp.exp(cum)).astype(BF16)
        kt = (kf * jnp.exp(-cum)).astype(BF16)
        ke = (kf * jnp.exp(tot - cum)).astype(BF16)
        yield
        att = [_dot_nt(qt[:, ks], kt[:, ks]) for ks in keys]
        yield
        v = v_ref[rows, :]
        intra = [_dot(jnp.where(tri, att[h], 0.0).astype(BF16), v[:, vals[h]]) for h in range(GLA_HEADS)]
        upd = [_dot_tn(v[:, vals[h]], ke[:, keys[h]]) for h in range(GLA_HEADS)]
        yield
        staged.append((rows, qt, jnp.exp(tot), intra, jnp.concatenate(upd, axis=1)))
    for rows, qt, etot, intra, upd in staged:
        st = st_ref[...]
        stb = st.astype(BF16)
        outs = [intra[h] + _dot_nt(qt[:, keys[h]], stb[:, keys[h]]) for h in range(GLA_HEADS)]
        st_ref[...] = st * etot + upd
        yield
        o_ref[rows, :] = jnp.concatenate(outs, axis=1).astype(o_ref.dtype)
        yield


def _gla_kernel(qf_ref, qb_ref, kf_ref, kb_ref, vf_ref, vb_ref, gf_ref, gb_ref, of_ref, ob_ref, stf_ref, stb_ref,
                *, nck, nseq):
    @pl.when(pl.program_id(1) == 0)
    def _():
        stf_ref[...] = jnp.zeros_like(stf_ref)
        stb_ref[...] = jnp.zeros_like(stb_ref)

    streams = []
    for s in range(nseq):
        streams.append(_gla_stream(qf_ref.at[s], kf_ref.at[s], vf_ref.at[s], gf_ref.at[s], of_ref.at[s],
                                   stf_ref.at[s], False, nck))
        streams.append(_gla_stream(qb_ref.at[s], kb_ref.at[s], vb_ref.at[s], gb_ref.at[s], ob_ref.at[s],
                                   stb_ref.at[s], True, nck))
    _interleave(streams)


def _gla_scan(q, k, v, gk, tb, nseq):
    bsz, seq, _ = q.shape
    nb = seq // tb

    def fwd(width, cb=0):
        return pl.BlockSpec((nseq, tb, width), lambda b, j: (b, j, cb))

    def bwd(width, cb=0):
        return pl.BlockSpec((nseq, tb, width), lambda b, j: (b, nb - 1 - j, cb))

    out = jax.ShapeDtypeStruct((bsz, seq, GLA_VAL), BF16)
    state = pltpu.VMEM((nseq, GLA_HV, GLA_KEY), F32)
    return pl.pallas_call(
        functools.partial(_gla_kernel, nck=tb // GLA_CHUNK, nseq=nseq),
        grid=(bsz // nseq, nb),
        in_specs=[fwd(GLA_KEY), bwd(GLA_KEY), fwd(GLA_KEY), bwd(GLA_KEY), fwd(GLA_VAL), bwd(GLA_VAL),
                  fwd(GLA_KEY, 0), bwd(GLA_KEY, 1)],
        out_specs=[fwd(GLA_VAL), bwd(GLA_VAL)],
        out_shape=[out, out],
        scratch_shapes=[state, state],
        compiler_params=_params(2),
        name="gla_scan",
    )(q, q, k, k, v, v, gk, gk)


def _gla_out_kernel(of_ref, ob_ref, g_ref, x_ref, gnw_ref, wo_ref, nw_ref, o_ref):
    o = of_ref[...].astype(F32) + ob_ref[...].astype(F32)
    gate = _silu(g_ref[...].astype(F32))
    gnw = gnw_ref[...]
    m = None
    for h in range(GLA_HEADS):
        hs = slice(h * GLA_HV, (h + 1) * GLA_HV)
        oh = (_rms(o[:, hs], gnw) * gate[:, hs]).astype(BF16)
        d = _dot(oh, wo_ref[hs, :])
        m = d if m is None else m + d
    o_ref[...] = x_ref[...] + _rms(m, nw_ref[...])


def _gla_out(of, ob, g, x, gnw, wo, nw, tm):
    t = x.shape[0]
    tok = pl.BlockSpec((tm, D_MODEL), lambda i: (i, 0))
    return pl.pallas_call(
        _gla_out_kernel,
        grid=(t // tm,),
        in_specs=[tok, tok, tok, tok, _const_spec(gnw.shape), _const_spec(wo.shape), _const_spec(nw.shape)],
        out_specs=tok,
        out_shape=jax.ShapeDtypeStruct(x.shape, F32),
        compiler_params=_params(1),
        name="gla_out",
    )(of, ob, g, x, gnw, wo, nw)


def _ffn_kernel(xm_ref, xp_ref, xn_ref, nw_ref, wg_ref, wu_ref, cwg_ref, cwu_ref, wo_ref, nwo_ref, o_ref,
                hcat_ref, *, tm, nt, nchunk):
    j = pl.program_id(1)
    _fill_halo_rows(hcat_ref, xm_ref, xp_ref, xn_ref, nw_ref[...], j, nt, tm)
    hcat = hcat_ref[...]

    def project(c):
        return _dot(hcat, wg_ref[c]), _dot(hcat, wu_ref[c])

    u_next = project(0)
    acc = None
    acts = []
    for c in range(nchunk):
        u_gate, u_up = u_next
        if c + 1 < nchunk:
            u_next = project(c + 1)
        gate = _conv3(u_gate, cwg_ref[c], tm)
        up = _conv3(u_up, cwu_ref[c], tm)
        acts.append((_gelu_tanh(gate) * up).astype(BF16))
        if len(acts) == 2 or c + 1 == nchunk:
            c0 = c + 1 - len(acts)
            act = acts[0] if len(acts) == 1 else jnp.concatenate(acts, axis=1)
            d = _dot(act, wo_ref[c0 * FFN_COL_CHUNK:(c + 1) * FFN_COL_CHUNK, :])
            acc = d if acc is None else acc + d
            acts = []
    o_ref[0] = xm_ref[0] + _rms(acc, nwo_ref[...])


def _ffn(x, nw, wg, wu, cwg, cwu, wo, nwo, tm):
    bsz, seq, _ = x.shape
    nt = seq // tm
    main, prev, nxt = _halo_specs(tm, seq)
    return pl.pallas_call(
        functools.partial(_ffn_kernel, tm=tm, nt=nt, nchunk=wg.shape[0]),
        grid=(bsz, nt),
        in_specs=[main, prev, nxt, _const_spec(nw.shape), _const_spec(wg.shape), _const_spec(wu.shape),
                  _const_spec(cwg.shape), _const_spec(cwu.shape), _const_spec(wo.shape), _const_spec(nwo.shape)],
        out_specs=pl.BlockSpec((1, tm, D_MODEL), lambda b, j: (b, j, 0)),
        out_shape=jax.ShapeDtypeStruct(x.shape, F32),
        scratch_shapes=[pltpu.VMEM((tm + HALO, D_MODEL), BF16)],
        compiler_params=_params(2),
        name="conv_ffn",
    )(x, x, x, nw, wg, wu, cwg, cwu, wo, nwo)


def _pad_cols(w, width):
    return jnp.pad(w, ((0, 0),) * (w.ndim - 1) + ((0, width - w.shape[-1]),))


def _swap_halves(w):
    half = w.shape[-1] // 2
    return jnp.concatenate([w[..., half:], w[..., :half]], axis=-1)


def _row(v):
    return v.reshape(1, -1).astype(F32)


def _prep_even(w_in, conv_w, conv_b, a_log, dt_bias, d_skip, ssd_norm_w, q_norm_w, w_qb, kv_norm_w, w_kvb, w_out):
    o1 = SSD_INNER
    o2 = o1 + SSD_CONV_DIM
    o3 = o2 + 2 * SSD_HEADS
    o4 = o3 + MLA_Q_LORA
    o5 = o4 + MLA_KV_LORA
    w_kr = w_in[:, o5:]
    wall = jnp.concatenate([w_in[:, :o2], w_in[:, o3:o5], _pad_cols(w_kr, LANES), _pad_cols(_swap_halves(w_kr), LANES),
                            _pad_cols(w_in[:, o2:o3], LANES)], axis=1).astype(BF16)
    cw = jnp.concatenate([conv_w, conv_b[None, :]], axis=0).astype(F32)
    dtb = _pad_cols(_row(dt_bias), LANES)
    alog = _pad_cols(_row(a_log), LANES)
    dskip = _row(jnp.repeat(d_skip, SSD_HEAD_DIM))
    wq3 = w_qb.reshape(MLA_Q_LORA, MLA_HEADS, MLA_NOPE + MLA_ROPE)
    rope = wq3[..., MLA_NOPE:]
    wq = jnp.concatenate([wq3[..., :MLA_NOPE].reshape(MLA_Q_LORA, -1),
                          _pad_cols(rope, LANES).reshape(MLA_Q_LORA, -1),
                          _pad_cols(_swap_halves(rope), LANES).reshape(MLA_Q_LORA, -1)], axis=1).astype(BF16)
    wkv3 = w_kvb.reshape(MLA_KV_LORA, MLA_HEADS, MLA_NOPE + MLA_V)
    wuk = jnp.transpose(wkv3[..., :MLA_NOPE], (1, 2, 0)).astype(BF16)
    wuv = jnp.transpose(wkv3[..., MLA_NOPE:], (1, 0, 2)).astype(BF16)
    return dict(wall=wall, cw=cw, dtb=dtb, alog=alog, dskip=dskip, snw=_row(ssd_norm_w), qnw=_row(q_norm_w), wq=wq,
                wuk=wuk, kvnw=_row(kv_norm_w), wuv=wuv, wo=w_out.astype(BF16))


def _rope_tables(seq):
    inv = 1.0 / (ROPE_THETA ** (jnp.arange(0, MLA_ROPE, 2, dtype=F32) / MLA_ROPE))
    ang = jnp.arange(seq, dtype=F32)[:, None] * inv[None, :]
    cos, sin = jnp.cos(ang), jnp.sin(ang)
    return (_pad_cols(jnp.concatenate([cos, cos], axis=1), LANES),
            _pad_cols(jnp.concatenate([-sin, sin], axis=1), LANES))


def _prep_gla(w_in, w_gk2, b_gk, norm_w, w_out):
    w = _pad_cols(w_in, 2 * GLA_KEY + 2 * GLA_VAL + LANES).astype(BF16)
    wgk = jnp.zeros((LANES, 2 * GLA_KEY), F32)
    for d in range(2):
        wgk = wgk.at[d * GLA_GATE_RANK:(d + 1) * GLA_GATE_RANK, d * GLA_KEY:(d + 1) * GLA_KEY].set(w_gk2[d])
    return dict(w=w, wgk=wgk.astype(BF16), bgk=_row(b_gk), gnw=_row(norm_w), wo=w_out.astype(BF16))


def _prep_ffn(w_in, conv_w, conv_b, w_out):
    nchunk = D_FF // FFN_COL_CHUNK

    def cols(w):
        return jnp.transpose(w.reshape(w.shape[0], nchunk, FFN_COL_CHUNK), (1, 0, 2))

    cw = jnp.concatenate([conv_w, conv_b[None, :], jnp.zeros((4, 2 * D_FF), F32)], axis=0).astype(F32)
    return dict(wg=cols(w_in[:, :D_FF]).astype(BF16), wu=cols(w_in[:, D_FF:]).astype(BF16),
                cwg=cols(cw[:, :D_FF]), cwu=cols(cw[:, D_FF:]),
                wo=w_out.astype(BF16))


def _tile(total, pref):
    return min(total, pref)


def _prepare(p):
    layers = []
    for layer in range(p['norm_w'].shape[0]):
        i = layer // 2
        if layer % 2 == 0:
            mixer = _prep_even(p['hyb_w_in'][i], p['ssd_conv_w'][i], p['ssd_conv_b'][i], p['ssd_a_log'][i],
                               p['ssd_dt_bias'][i], p['ssd_d'][i], p['ssd_norm_w'][i], p['mla_q_norm_w'][i],
                               p['mla_w_qb'][i], p['mla_kv_norm_w'][i], p['mla_w_kvb'][i], p['hyb_w_out'][i])
        else:
            mixer = _prep_gla(p['gla_w_in'][i], p['gla_w_gk2'][i], p['gla_b_gk'][i], p['gla_norm_w'][i],
                              p['gla_w_out'][i])
        ffn = _prep_ffn(p['ffn_w_in'][layer], p['ffn_conv_w'][layer], p['ffn_conv_b'][layer], p['ffn_w_out'][layer])
        layers.append((p['norm_w'][layer].astype(F32), mixer, ffn))
    return layers


def _trunk(x, layers, tiles=None):
    tiles = dict(dict(even_in=512, ssd=512, ssd_seqs=1, mla=256, tok=512, gla=256, gla_seqs=2, ffn=1024), **(tiles or {}))
    bsz, seq, d = x.shape
    cos, sin = _rope_tables(seq)
    flat = lambda t: t.reshape(bsz * seq, t.shape[-1])
    for layer, (nw, e, f) in enumerate(layers):
        if layer % 2 == 0:
            z, xs, bc, dt, q, k = _even_in(x, nw[0:1], e['wall'], e['cw'], e['dtb'], e['qnw'], e['wq'], e['wuk'],
                                           e['kvnw'], cos, sin, _tile(seq, tiles['even_in']))
            yf, yb = _ssd_scan(xs, bc, dt, e['alog'], _tile(seq, tiles['ssd']), math.gcd(bsz, tiles['ssd_seqs']))
            om = _mla_attention(q, k, e['wuv'], _tile(seq, tiles['mla']))
            x = _even_out(flat(yf), flat(yb), flat(xs), flat(z), flat(om), flat(x), e['dskip'], e['snw'], e['wo'],
                          nw[1:2], _tile(bsz * seq, tiles['tok'])).reshape(bsz, seq, d)
        else:
            q, k, v, g, gk = _gla_in(flat(x), nw[0:1], e['w'], e['wgk'], e['bgk'], _tile(bsz * seq, tiles['tok']))
            r3 = lambda t: t.reshape(bsz, seq, t.shape[-1])
            of, ob = _gla_scan(r3(q), r3(k), r3(v), r3(gk), _tile(seq, tiles['gla']), math.gcd(bsz, tiles['gla_seqs']))
            x = _gla_out(flat(of), flat(ob), g, flat(x), e['gnw'], e['wo'], nw[1:2],
                         _tile(bsz * seq, tiles['tok'])).reshape(bsz, seq, d)
        x = _ffn(x, nw[2:3], f['wg'], f['wu'], f['cwg'], f['cwu'], f['wo'], nw[3:4], _tile(seq, tiles['ffn']))
    return x


def kernel(x_prompt, x_sample, hyb_w_in, ssd_conv_w, ssd_conv_b, ssd_a_log, ssd_dt_bias, ssd_d, ssd_norm_w, mla_q_norm_w, mla_w_qb, mla_kv_norm_w, mla_w_kvb, hyb_w_out, gla_w_in, gla_w_gk2, gla_b_gk, gla_norm_w, gla_w_out, ffn_w_in, ffn_conv_w, ffn_conv_b, ffn_w_out, norm_w):
    params = dict(hyb_w_in=hyb_w_in, ssd_conv_w=ssd_conv_w, ssd_conv_b=ssd_conv_b, ssd_a_log=ssd_a_log,
                  ssd_dt_bias=ssd_dt_bias, ssd_d=ssd_d, ssd_norm_w=ssd_norm_w,
                  mla_q_norm_w=mla_q_norm_w, mla_w_qb=mla_w_qb, mla_kv_norm_w=mla_kv_norm_w,
                  mla_w_kvb=mla_w_kvb, hyb_w_out=hyb_w_out,
                  gla_w_in=gla_w_in, gla_w_gk2=gla_w_gk2, gla_b_gk=gla_b_gk, gla_norm_w=gla_norm_w,
                  gla_w_out=gla_w_out, ffn_w_in=ffn_w_in, ffn_conv_w=ffn_conv_w, ffn_conv_b=ffn_conv_b,
                  ffn_w_out=ffn_w_out, norm_w=norm_w)
    layers = _prepare(params)
    return (_trunk(x_prompt, layers), _trunk(x_sample, layers))
```

```python
import functools
import math

import jax
import jax.numpy as jnp
from jax import lax
from jax.experimental import pallas as pl
from jax.experimental.pallas import tpu as pltpu

F32 = jnp.float32
BF16 = jnp.bfloat16
LOG2_E = math.log2(math.e)

D_MODEL = 1024
EPS = 1e-6

SSD_HEADS = 16
SSD_HEAD_DIM = 64
SSD_INNER = SSD_HEADS * SSD_HEAD_DIM
SSD_GROUPS = 2
SSD_HPG = SSD_HEADS // SSD_GROUPS
SSD_STATE = 128
SSD_CHUNK = 128
SSD_CONV_DIM = SSD_INNER + 2 * SSD_GROUPS * SSD_STATE
SSD_GROUP_WIDTH = SSD_INNER // SSD_GROUPS

MLA_HEADS = 8
MLA_Q_LORA = 256
MLA_KV_LORA = 128
MLA_NOPE = 128
MLA_ROPE = 64
MLA_V = 128
ROPE_THETA = 10000.0
MLA_QK_WIDTH = 256
MLA_GROUP = 4

GLA_HEADS = 4
GLA_KEY = D_MODEL // 2
GLA_VAL = D_MODEL
GLA_HK = GLA_KEY // GLA_HEADS
GLA_HV = GLA_VAL // GLA_HEADS
GLA_GATE_RANK = 16
GLA_GATE_NORM = 16.0
GLA_CHUNK = 64

D_FF = 2816
FFN_COL_CHUNK = 256

LANES = 128
SUBLANES = 8
HALO = 2 * SUBLANES
VMEM_LIMIT_BYTES = 56 * 1024 * 1024

_E_Z = 0
_E_XBC = _E_Z + SSD_INNER
_E_QA = _E_XBC + SSD_CONV_DIM
_E_CKV = _E_QA + MLA_Q_LORA
_E_KR = _E_CKV + MLA_KV_LORA
_E_KRSW = _E_KR + LANES
_E_DT = _E_KRSW + LANES
_E_END = _E_DT + LANES


def _dot(a, b):
    return jnp.dot(a, b, preferred_element_type=F32)


def _dot_nt(a, b):
    return lax.dot_general(a, b, (((1,), (1,)), ((), ())), preferred_element_type=F32)


def _dot_tn(a, b):
    return lax.dot_general(a, b, (((0,), (0,)), ((), ())), preferred_element_type=F32)


def _mask_dot(mask, v):
    t = mask.astype(BF16)
    h1 = v.astype(BF16)
    r1 = v - h1.astype(F32)
    h2 = r1.astype(BF16)
    h3 = (r1 - h2.astype(F32)).astype(BF16)
    if mask.shape[1] % LANES == 0:
        return _dot(jnp.concatenate([t, t, t], axis=1), jnp.concatenate([h1, h2, h3], axis=0))
    return _dot(t, h1) + _dot(t, h2) + _dot(t, h3)


def _rms(x, w):
    return x * lax.rsqrt(jnp.mean(x * x, axis=-1, keepdims=True) + EPS) * w


def _softplus(x):
    return jnp.maximum(x, 0.0) + jnp.log1p(jnp.exp(-jnp.abs(x)))


def _silu(x):
    return x * jax.nn.sigmoid(x)


def _gelu_tanh(x):
    k = -2.0 * math.sqrt(2.0 / math.pi) * LOG2_E
    return x / (1.0 + jnp.exp2((x * x * (k * 0.044715) + k) * x))


def _params(n_axes, flags=None):
    return pltpu.CompilerParams(dimension_semantics=("arbitrary",) * n_axes,
                                vmem_limit_bytes=VMEM_LIMIT_BYTES, flags=flags)


def _interleave(streams, skew=0):
    pending = list(enumerate(streams))
    rnd = 0
    while pending:
        alive = []
        for i, stream in pending:
            if rnd >= skew * i:
                try:
                    next(stream)
                except StopIteration:
                    continue
            alive.append((i, stream))
        pending = alive
        rnd += 1


def _const_spec(shape):
    nd = len(shape)
    return pl.BlockSpec(shape, lambda *_: (0,) * nd, pipeline_mode=pl.Buffered(1))


def _halo_specs(tm, seq):
    per = tm // SUBLANES
    last = seq // SUBLANES - 1
    main = pl.BlockSpec((1, tm, D_MODEL), lambda b, j: (b, j, 0))
    prev = pl.BlockSpec((1, SUBLANES, D_MODEL), lambda b, j: (b, jnp.maximum(j * per - 1, 0), 0))
    nxt = pl.BlockSpec((1, SUBLANES, D_MODEL), lambda b, j: (b, jnp.minimum((j + 1) * per, last), 0))
    return main, prev, nxt


def _fill_halo_rows(hcat_ref, xm_ref, xp_ref, xn_ref, nw, j, nt, tm):
    hn = jnp.where(j == nt - 1, 0.0, _rms(xn_ref[0], nw))
    hp = jnp.where(j == 0, 0.0, _rms(xp_ref[0], nw))
    hcat_ref[0:HALO, :] = jnp.concatenate([hn, hp], axis=0).astype(BF16)
    hcat_ref[HALO:, :] = _rms(xm_ref[0], nw).astype(BF16)


def _conv3(u, cw, tm):
    rows = tm + HALO
    um1 = pltpu.roll(u, 1, 0)[HALO:]
    up1 = pltpu.roll(u, rows - 1, 0)[HALO:]
    return um1 * cw[0:1] + u[HALO:] * cw[1:2] + up1 * cw[2:3] + cw[3:4]


def _even_in_kernel(xm_ref, xp_ref, xn_ref, nw_ref, wall_ref, cw_ref, dtb_ref, qnw_ref, wq_ref, wuk_ref,
                    kvnw_ref, cos_ref, sin_ref,
                    z_ref, xs_ref, bc_ref, dt_ref, q_ref, k_ref, hcat_ref, *, tm, nt):
    j = pl.program_id(1)
    _fill_halo_rows(hcat_ref, xm_ref, xp_ref, xn_ref, nw_ref[...], j, nt, tm)
    hcat = hcat_ref[...]
    hm = hcat_ref[HALO:, :]

    def ssd_columns():
        for c0 in range(0, SSD_CONV_DIM, 512):
            u = _dot(hcat, wall_ref[:, _E_XBC + c0:_E_XBC + c0 + 512])
            yield
            y = _silu(_conv3(u, cw_ref[:, c0:c0 + 512], tm)).astype(BF16)
            if c0 < SSD_INNER:
                xs_ref[0, :, c0:c0 + 512] = y
            else:
                bc_ref[0] = y
            yield

    def gate_columns():
        half = SSD_INNER // 2
        for c0 in (0, half):
            z = _dot(hm, wall_ref[:, _E_Z + c0:_E_Z + c0 + half])
            yield
            z_ref[0, :, c0:c0 + half] = z.astype(BF16)
            yield

    def mla_columns():
        rest = _dot(hm, wall_ref[:, _E_QA:_E_END])
        yield
        o = -_E_QA
        qa = rest[:, o + _E_QA:o + _E_CKV]
        ckv = rest[:, o + _E_CKV:o + _E_KR]
        kr = rest[:, o + _E_KR:o + _E_KRSW]
        krsw = rest[:, o + _E_KRSW:o + _E_DT]
        dtr = rest[:, o + _E_DT:o + _E_END]
        cos = cos_ref[...]
        sin = sin_ref[...]
        scale = (MLA_NOPE + MLA_ROPE) ** -0.5 * LOG2_E
        hq = _rms(qa, qnw_ref[...]).astype(BF16)
        nh = MLA_HEADS * LANES
        qall = _dot(hq, wq_ref[...])
        yield
        dt_ref[0] = _softplus(dtr + dtb_ref[...])
        k_ref[0, :, 0:LANES] = _rms(ckv, kvnw_ref[...]).astype(BF16)
        k_ref[0, :, LANES:2 * LANES] = (kr * cos + krsw * sin).astype(BF16)
        yield
        for h in range(MLA_HEADS):
            qn = qall[:, h * LANES:(h + 1) * LANES].astype(BF16)
            ql = _dot(qn, wuk_ref[h])
            qr = (qall[:, nh + h * LANES:nh + (h + 1) * LANES] * cos
                  + qall[:, 2 * nh + h * LANES:2 * nh + (h + 1) * LANES] * sin)
            q_ref[0, h, :, 0:LANES] = (ql * scale).astype(BF16)
            q_ref[0, h, :, LANES:MLA_QK_WIDTH] = (qr * scale).astype(BF16)
            if h % 2 == 1:
                yield

    _interleave([mla_columns(), ssd_columns(), gate_columns()])


def _even_in(x, nw, wall, cw, dtb, qnw, wq, wuk, kvnw, cos, sin, tm):
    bsz, seq, _ = x.shape
    nt = seq // tm
    main, prev, nxt = _halo_specs(tm, seq)

    def tok(width):
        return pl.BlockSpec((1, tm, width), lambda b, j: (b, j, 0))

    def out(width, dtype):
        return jax.ShapeDtypeStruct((bsz, seq, width), dtype)

    rope_spec = pl.BlockSpec((tm, LANES), lambda b, j: (j, 0))
    return pl.pallas_call(
        functools.partial(_even_in_kernel, tm=tm, nt=nt),
        grid=(bsz, nt),
        in_specs=[main, prev, nxt, _const_spec(nw.shape), _const_spec(wall.shape), _const_spec(cw.shape),
                  _const_spec(dtb.shape), _const_spec(qnw.shape), _const_spec(wq.shape), _const_spec(wuk.shape),
                  _const_spec(kvnw.shape), rope_spec, rope_spec],
        out_specs=[tok(SSD_INNER), tok(SSD_INNER), tok(2 * SSD_GROUPS * SSD_STATE), tok(LANES),
                   pl.BlockSpec((1, MLA_HEADS, tm, MLA_QK_WIDTH), lambda b, j: (b, 0, j, 0)), tok(MLA_QK_WIDTH)],
        out_shape=[out(SSD_INNER, BF16), out(SSD_INNER, BF16), out(2 * SSD_GROUPS * SSD_STATE, BF16),
                   out(LANES, F32), jax.ShapeDtypeStruct((bsz, MLA_HEADS, seq, MLA_QK_WIDTH), BF16),
                   out(MLA_QK_WIDTH, BF16)],
        scratch_shapes=[pltpu.VMEM((tm + HALO, D_MODEL), BF16)],
        compiler_params=_params(2),
        name="even_in",
    )(x, x, x, nw, wall, cw, dtb, qnw, wq, wuk, kvnw, cos, sin)


def _expand_heads(v, expand2):
    hi = v.astype(BF16)
    lo = (v - hi.astype(F32)).astype(BF16)
    return _dot(jnp.concatenate([hi, lo], axis=1), expand2)


def _ssd_stream(x_ref, bc_ref, dt_ref, y_ref, st_ref, a, reverse, lane0, nck):
    L = SSD_CHUNK
    row = lax.broadcasted_iota(jnp.int32, (L, L), 0)
    col = lax.broadcasted_iota(jnp.int32, (L, L), 1)
    tri = (row <= col) if reverse else (row >= col)
    er = lax.broadcasted_iota(jnp.int32, (LANES, SSD_INNER), 0)
    ec = lax.broadcasted_iota(jnp.int32, (LANES, SSD_INNER), 1)
    head_of_lane = lax.shift_right_logical(ec, int(math.log2(SSD_HEAD_DIM)))
    expand = jnp.where(er - lane0 == head_of_lane, 1.0, 0.0).astype(BF16)
    expand = jnp.concatenate([expand, expand], axis=0)
    lane = lax.broadcasted_iota(jnp.int32, (L, LANES), 1)
    low_half = lane < SSD_HEAD_DIM
    gw = SSD_GROUP_WIDTH
    groups = [slice(g * gw, (g + 1) * gw) for g in range(SSD_GROUPS)]
    staged = []
    for ci in (range(nck - 1, -1, -1) if reverse else range(nck)):
        rows = pl.ds(ci * L, L)
        dt = dt_ref[rows, :]
        cum = _mask_dot(tri, dt * a)
        yield
        shifted_t = (cum - jnp.log(dt)).T
        tot = cum[0:1] if reverse else cum[L - 1:L]
        ecum = _expand_heads(jnp.exp(cum), expand)
        wend = _expand_heads(jnp.exp(tot - cum) * dt, expand)
        etot = _expand_heads(jnp.broadcast_to(jnp.exp(tot), (SUBLANES, LANES)), expand)[0:1]
        yield
        x = x_ref[rows, :]
        bc = bc_ref[rows, :]
        xw = (x.astype(F32) * wend).astype(BF16)
        bs = [bc[:, g * SSD_STATE:(g + 1) * SSD_STATE] for g in range(SSD_GROUPS)]
        cs = [bc[:, (SSD_GROUPS + g) * SSD_STATE:(SSD_GROUPS + g + 1) * SSD_STATE] for g in range(SSD_GROUPS)]
        cb = [_dot_nt(cs[g], bs[g]) for g in range(SSD_GROUPS)]
        upd = jnp.concatenate([_dot_tn(bs[g], xw[:, groups[g]]) for g in range(SSD_GROUPS)], axis=1)
        yield
        intra = []
        for g in range(SSD_GROUPS):
            for p in range(SSD_HPG // 2):
                lo = g * gw + p * LANES
                x_pair = x[:, lo:lo + LANES]
                zero = jnp.zeros_like(x_pair)
                x_stack = jnp.concatenate([jnp.where(low_half, x_pair, zero), jnp.where(low_half, zero, x_pair)],
                                          axis=0)
                pair = []
                for e in (2 * p, 2 * p + 1):
                    c = lane0 + g * SSD_HPG + e
                    diff = cum[:, c:c + 1] - shifted_t[c:c + 1, :]
                    pair.append((cb[g] * jnp.exp(jnp.where(tri, diff, -jnp.inf))).astype(BF16))
                intra.append(_dot(jnp.concatenate(pair, axis=1), x_stack))
            yield
        staged.append((rows, cs, ecum, etot, intra, upd))
    for rows, cs, ecum, etot, intra, upd in staged:
        st = st_ref[...]
        stb = st.astype(BF16)
        ys = []
        for g in range(SSD_GROUPS):
            y_inter = _dot(cs[g], stb[:, groups[g]])
            for p in range(SSD_HPG // 2):
                lo = g * gw + p * LANES
                ys.append(intra[g * (SSD_HPG // 2) + p] + y_inter[:, p * LANES:(p + 1) * LANES] * ecum[:, lo:lo + LANES])
        st_ref[...] = st * etot + upd
        yield
        y_ref[rows, :] = jnp.concatenate(ys, axis=1).astype(y_ref.dtype)
        yield


def _ssd_kernel(xf_ref, xb_ref, bcf_ref, bcb_ref, dtf_ref, dtb_ref, alog_ref, yf_ref, yb_ref, stf_ref, stb_ref,
                *, nck, nseq):
    @pl.when(pl.program_id(1) == 0)
    def _():
        stf_ref[...] = jnp.zeros_like(stf_ref)
        stb_ref[...] = jnp.zeros_like(stb_ref)

    a = -jnp.exp(alog_ref[...])
    streams = []
    for s in range(nseq):
        streams.append(_ssd_stream(xf_ref.at[s], bcf_ref.at[s], dtf_ref.at[s], yf_ref.at[s], stf_ref.at[s], a,
                                   False, 0, nck))
        streams.append(_ssd_stream(xb_ref.at[s], bcb_ref.at[s], dtb_ref.at[s], yb_ref.at[s], stb_ref.at[s], a,
                                   True, SSD_HEADS, nck))
    _interleave(streams)


def _ssd_scan(xs, bc, dt, alog, tb, nseq):
    bsz, seq, _ = xs.shape
    nb = seq // tb

    def fwd(width):
        return pl.BlockSpec((nseq, tb, width), lambda b, j: (b, j, 0))

    def bwd(width):
        return pl.BlockSpec((nseq, tb, width), lambda b, j: (b, nb - 1 - j, 0))

    out = jax.ShapeDtypeStruct((bsz, seq, SSD_INNER), BF16)
    state = pltpu.VMEM((nseq, SSD_STATE, SSD_INNER), F32)
    return pl.pallas_call(
        functools.partial(_ssd_kernel, nck=tb // SSD_CHUNK, nseq=nseq),
        grid=(bsz // nseq, nb),
        in_specs=[fwd(SSD_INNER), bwd(SSD_INNER), fwd(bc.shape[-1]), bwd(bc.shape[-1]), fwd(LANES), bwd(LANES),
                  _const_spec(alog.shape)],
        out_specs=[fwd(SSD_INNER), bwd(SSD_INNER)],
        out_shape=[out, out],
        scratch_shapes=[state, state],
        compiler_params=_params(2),
        name="ssd_scan",
    )(xs, xs, bc, bc, dt, dt, alog)


def _mla_kernel(q_ref, k_ref, wuv_ref, o_ref, *, tq):
    k = k_ref[0]
    v_t = jnp.concatenate([k[:, 0:MLA_KV_LORA].astype(F32).T.astype(BF16), jnp.ones((HALO, k.shape[0]), BF16)], axis=0)

    def group(g):
        q = q_ref[0, g * MLA_GROUP:(g + 1) * MLA_GROUP].reshape(MLA_GROUP * tq, MLA_QK_WIDTH)
        s = _dot_nt(q, k)
        yield
        m = jnp.max(s, axis=-1, keepdims=True)
        p = jnp.exp2(s - m).astype(BF16)
        o_t = _dot_nt(v_t, p)
        yield
        o_lat_t = o_t[0:MLA_KV_LORA] / o_t[MLA_KV_LORA:MLA_KV_LORA + 1]
        for i in range(MLA_GROUP):
            h = g * MLA_GROUP + i
            o_lat = o_lat_t[:, i * tq:(i + 1) * tq].T.astype(BF16)
            o_ref[0, :, h * MLA_V:(h + 1) * MLA_V] = _dot(o_lat, wuv_ref[h]).astype(o_ref.dtype)
        yield

    _interleave([group(g) for g in range(MLA_HEADS // MLA_GROUP)])


def _mla_attention(q, k, wuv, tq):
    bsz, _, seq, _ = q.shape
    return pl.pallas_call(
        functools.partial(_mla_kernel, tq=tq),
        grid=(bsz, seq // tq),
        in_specs=[pl.BlockSpec((1, MLA_HEADS, tq, MLA_QK_WIDTH), lambda b, j: (b, 0, j, 0)),
                  pl.BlockSpec((1, seq, k.shape[-1]), lambda b, j: (b, 0, 0)),
                  _const_spec(wuv.shape)],
        out_specs=pl.BlockSpec((1, tq, MLA_HEADS * MLA_V), lambda b, j: (b, j, 0)),
        out_shape=jax.ShapeDtypeStruct((bsz, seq, MLA_HEADS * MLA_V), BF16),
        compiler_params=_params(2),
        name="mla_attention",
    )(q, k, wuv)


def _even_out_kernel(yf_ref, yb_ref, xs_ref, z_ref, om_ref, x_ref, dskip_ref, snw_ref, wo_ref, nw_ref, o_ref):
    y = yf_ref[...].astype(F32) + yb_ref[...].astype(F32) + xs_ref[...].astype(F32) * dskip_ref[...]
    y = y * _silu(z_ref[...].astype(F32))
    snw = snw_ref[...]
    gw = SSD_GROUP_WIDTH
    m = _dot(om_ref[...], wo_ref[SSD_INNER:, :])
    for g in range(SSD_GROUPS):
        yg = _rms(y[:, g * gw:(g + 1) * gw], snw[:, g * gw:(g + 1) * gw]).astype(BF16)
        m = m + _dot(yg, wo_ref[g * gw:(g + 1) * gw, :])
    o_ref[...] = x_ref[...] + _rms(m, nw_ref[...])


def _even_out(yf, yb, xs, z, om, x, dskip, snw, wo, nw, tm):
    t = x.shape[0]
    tok = pl.BlockSpec((tm, D_MODEL), lambda i: (i, 0))
    return pl.pallas_call(
        _even_out_kernel,
        grid=(t // tm,),
        in_specs=[tok, tok, tok, tok, tok, tok, _const_spec(dskip.shape), _const_spec(snw.shape),
                  _const_spec(wo.shape), _const_spec(nw.shape)],
        out_specs=tok,
        out_shape=jax.ShapeDtypeStruct(x.shape, F32),
        compiler_params=_params(1),
        name="even_out",
    )(yf, yb, xs, z, om, x, dskip, snw, wo, nw)


def _gla_in_kernel(x_ref, nw_ref, w_ref, wgk_ref, bgk_ref, q_ref, k_ref, v_ref, g_ref, gk_ref):
    h = _rms(x_ref[...], nw_ref[...]).astype(BF16)
    o1, o2, o3, o4 = GLA_KEY, 2 * GLA_KEY, 2 * GLA_KEY + GLA_VAL, 2 * GLA_KEY + 2 * GLA_VAL

    def gates():
        lr = _dot(h, w_ref[:, o4:]).astype(BF16)
        yield
        pre = _dot(lr, wgk_ref[...]) + bgk_ref[...]
        yield
        gk_ref[...] = (jnp.minimum(pre, 0.0) - jnp.log1p(jnp.exp(-jnp.abs(pre)))) / GLA_GATE_NORM
        yield

    def projections():
        q_ref[...] = (_dot(h, w_ref[:, 0:o1]) * GLA_HK ** -0.5).astype(BF16)
        yield
        k_ref[...] = _dot(h, w_ref[:, o1:o2]).astype(BF16)
        yield
        v_ref[...] = _dot(h, w_ref[:, o2:o3]).astype(BF16)
        yield
        g_ref[...] = _dot(h, w_ref[:, o3:o4]).astype(BF16)
        yield

    _interleave([gates(), projections()])


def _gla_in(x, nw, w, wgk, bgk, tm):
    t = x.shape[0]

    def tok(width):
        return pl.BlockSpec((tm, width), lambda i: (i, 0))

    def out(width, dtype):
        return jax.ShapeDtypeStruct((t, width), dtype)

    return pl.pallas_call(
        _gla_in_kernel,
        grid=(t // tm,),
        in_specs=[tok(D_MODEL), _const_spec(nw.shape), _const_spec(w.shape), _const_spec(wgk.shape),
                  _const_spec(bgk.shape)],
        out_specs=[tok(GLA_KEY), tok(GLA_KEY), tok(GLA_VAL), tok(GLA_VAL), tok(2 * GLA_KEY)],
        out_shape=[out(GLA_KEY, BF16), out(GLA_KEY, BF16), out(GLA_VAL, BF16), out(GLA_VAL, BF16),
                   out(2 * GLA_KEY, F32)],
        compiler_params=_params(1),
        name="gla_in",
    )(x, nw, w, wgk, bgk)


def _gla_stream(q_ref, k_ref, v_ref, g_ref, o_ref, st_ref, reverse, nck):
    L = GLA_CHUNK
    row = lax.broadcasted_iota(jnp.int32, (L, L), 0)
    col = lax.broadcasted_iota(jnp.int32, (L, L), 1)
    tri = (row <= col) if reverse else (row >= col)
    keys = [slice(h * GLA_HK, (h + 1) * GLA_HK) for h in range(GLA_HEADS)]
    vals = [slice(h * GLA_HV, (h + 1) * GLA_HV) for h in range(GLA_HEADS)]
    staged = []
    for ci in (range(nck - 1, -1, -1) if reverse else range(nck)):
        rows = pl.ds(ci * L, L)
        cum = _mask_dot(tri, g_ref[rows, :])
        yield
        tot = cum[0:1] if reverse else cum[L - 1:L]
        qf = q_ref[rows, :].astype(F32)
        kf = k_ref[rows, :].astype(F32)
        qt = (qf * jnp.exp(cum)).astype(BF16)
        kt = (kf * jnp.exp(-cum)).astype(BF16)
        ke = (kf * jnp.exp(tot - cum)).astype(BF16)
        yield
        att = [_dot_nt(qt[:, ks], kt[:, ks]) for ks in keys]
        yield
        v = v_ref[rows, :]
        intra = [_dot(jnp.where(tri, att[h], 0.0).astype(BF16), v[:, vals[h]]) for h in range(GLA_HEADS)]
        upd = [_dot_tn(v[:, vals[h]], ke[:, keys[h]]) for h in range(GLA_HEADS)]
        yield
        staged.append((rows, qt, jnp.exp(tot), intra, jnp.concatenate(upd, axis=1)))
    for rows, qt, etot, intra, upd in staged:
        st = st_ref[...]
        stb = st.astype(BF16)
        outs = [intra[h] + _dot_nt(qt[:, keys[h]], stb[:, keys[h]]) for h in range(GLA_HEADS)]
        st_ref[...] = st * etot + upd
        yield
        o_ref[rows, :] = jnp.concatenate(outs, axis=1).astype(o_ref.dtype)
        yield


def _gla_kernel(qf_ref, qb_ref, kf_ref, kb_ref, vf_ref, vb_ref, gf_ref, gb_ref, of_ref, ob_ref, stf_ref, stb_ref,
                *, nck, nseq):
    @pl.when(pl.program_id(1) == 0)
    def _():
        stf_ref[...] = jnp.zeros_like(stf_ref)
        stb_ref[...] = jnp.zeros_like(stb_ref)

    streams = []
    for s in range(nseq):
        streams.append(_gla_stream(qf_ref.at[s], kf_ref.at[s], vf_ref.at[s], gf_ref.at[s], of_ref.at[s],
                                   stf_ref.at[s], False, nck))
        streams.append(_gla_stream(qb_ref.at[s], kb_ref.at[s], vb_ref.at[s], gb_ref.at[s], ob_ref.at[s],
                                   stb_ref.at[s], True, nck))
    _interleave(streams)


def _gla_scan(q, k, v, gk, tb, nseq):
    bsz, seq, _ = q.shape
    nb = seq // tb

    def fwd(width, cb=0):
        return pl.BlockSpec((nseq, tb, width), lambda b, j: (b, j, cb))

    def bwd(width, cb=0):
        return pl.BlockSpec((nseq, tb, width), lambda b, j: (b, nb - 1 - j, cb))

    out = jax.ShapeDtypeStruct((bsz, seq, GLA_VAL), BF16)
    state = pltpu.VMEM((nseq, GLA_HV, GLA_KEY), F32)
    return pl.pallas_call(
        functools.partial(_gla_kernel, nck=tb // GLA_CHUNK, nseq=nseq),
        grid=(bsz // nseq, nb),
        in_specs=[fwd(GLA_KEY), bwd(GLA_KEY), fwd(GLA_KEY), bwd(GLA_KEY), fwd(GLA_VAL), bwd(GLA_VAL),
                  fwd(GLA_KEY, 0), bwd(GLA_KEY, 1)],
        out_specs=[fwd(GLA_VAL), bwd(GLA_VAL)],
        out_shape=[out, out],
        scratch_shapes=[state, state],
        compiler_params=_params(2),
        name="gla_scan",
    )(q, q, k, k, v, v, gk, gk)


def _gla_out_kernel(of_ref, ob_ref, g_ref, x_ref, gnw_ref, wo_ref, nw_ref, o_ref):
    o = of_ref[...].astype(F32) + ob_ref[...].astype(F32)
    gate = _silu(g_ref[...].astype(F32))
    gnw = gnw_ref[...]
    m = None
    for h in range(GLA_HEADS):
        hs = slice(h * GLA_HV, (h + 1) * GLA_HV)
        oh = (_rms(o[:, hs], gnw) * gate[:, hs]).astype(BF16)
        d = _dot(oh, wo_ref[hs, :])
        m = d if m is None else m + d
    o_ref[...] = x_ref[...] + _rms(m, nw_ref[...])


def _gla_out(of, ob, g, x, gnw, wo, nw, tm):
    t = x.shape[0]
    tok = pl.BlockSpec((tm, D_MODEL), lambda i: (i, 0))
    return pl.pallas_call(
        _gla_out_kernel,
        grid=(t // tm,),
        in_specs=[tok, tok, tok, tok, _const_spec(gnw.shape), _const_spec(wo.shape), _const_spec(nw.shape)],
        out_specs=tok,
        out_shape=jax.ShapeDtypeStruct(x.shape, F32),
        compiler_params=_params(1),
        name="gla_out",
    )(of, ob, g, x, gnw, wo, nw)


def _ffn_kernel(xm_ref, xp_ref, xn_ref, nw_ref, wg_ref, wu_ref, cwg_ref, cwu_ref, wo_ref, nwo_ref, o_ref,
                hcat_ref, *, tm, nt, nchunk):
    j = pl.program_id(1)
    _fill_halo_rows(hcat_ref, xm_ref, xp_ref, xn_ref, nw_ref[...], j, nt, tm)
    hcat = hcat_ref[...]

    def project(c):
        return _dot(hcat, wg_ref[c]), _dot(hcat, wu_ref[c])

    u_next = project(0)
    acc = None
    acts = []
    for c in range(nchunk):
        u_gate, u_up = u_next
        if c + 1 < nchunk:
            u_next = project(c + 1)
        gate = _conv3(u_gate, cwg_ref[c], tm)
        up = _conv3(u_up, cwu_ref[c], tm)
        acts.append((_gelu_tanh(gate) * up).astype(BF16))
        if len(acts) == 2 or c + 1 == nchunk:
            c0 = c + 1 - len(acts)
            act = acts[0] if len(acts) == 1 else jnp.concatenate(acts, axis=1)
            d = _dot(act, wo_ref[c0 * FFN_COL_CHUNK:(c + 1) * FFN_COL_CHUNK, :])
            acc = d if acc is None else acc + d
            acts = []
    o_ref[0] = xm_ref[0] + _rms(acc, nwo_ref[...])


def _ffn(x, nw, wg, wu, cwg, cwu, wo, nwo, tm):
    bsz, seq, _ = x.shape
    nt = seq // tm
    main, prev, nxt = _halo_specs(tm, seq)
    return pl.pallas_call(
        functools.partial(_ffn_kernel, tm=tm, nt=nt, nchunk=wg.shape[0]),
        grid=(bsz, nt),
        in_specs=[main, prev, nxt, _const_spec(nw.shape), _const_spec(wg.shape), _const_spec(wu.shape),
                  _const_spec(cwg.shape), _const_spec(cwu.shape), _const_spec(wo.shape), _const_spec(nwo.shape)],
        out_specs=pl.BlockSpec((1, tm, D_MODEL), lambda b, j: (b, j, 0)),
        out_shape=jax.ShapeDtypeStruct(x.shape, F32),
        scratch_shapes=[pltpu.VMEM((tm + HALO, D_MODEL), BF16)],
        compiler_params=_params(2),
        name="conv_ffn",
    )(x, x, x, nw, wg, wu, cwg, cwu, wo, nwo)


def _pad_cols(w, width):
    return jnp.pad(w, ((0, 0),) * (w.ndim - 1) + ((0, width - w.shape[-1]),))


def _swap_halves(w):
    half = w.shape[-1] // 2
    return jnp.concatenate([w[..., half:], w[..., :half]], axis=-1)


def _row(v):
    return v.reshape(1, -1).astype(F32)


def _prep_even(w_in, conv_w, conv_b, a_log, dt_bias, d_skip, ssd_norm_w, q_norm_w, w_qb, kv_norm_w, w_kvb, w_out):
    o1 = SSD_INNER
    o2 = o1 + SSD_CONV_DIM
    o3 = o2 + 2 * SSD_HEADS
    o4 = o3 + MLA_Q_LORA
    o5 = o4 + MLA_KV_LORA
    w_kr = w_in[:, o5:]
    wall = jnp.concatenate([w_in[:, :o2], w_in[:, o3:o5], _pad_cols(w_kr, LANES), _pad_cols(_swap_halves(w_kr), LANES),
                            _pad_cols(w_in[:, o2:o3], LANES)], axis=1).astype(BF16)
    cw = jnp.concatenate([conv_w, conv_b[None, :]], axis=0).astype(F32)
    dtb = _pad_cols(_row(dt_bias), LANES)
    alog = _pad_cols(_row(a_log), LANES)
    dskip = _row(jnp.repeat(d_skip, SSD_HEAD_DIM))
    wq3 = w_qb.reshape(MLA_Q_LORA, MLA_HEADS, MLA_NOPE + MLA_ROPE)
    rope = wq3[..., MLA_NOPE:]
    wq = jnp.concatenate([wq3[..., :MLA_NOPE].reshape(MLA_Q_LORA, -1),
                          _pad_cols(rope, LANES).reshape(MLA_Q_LORA, -1),
                          _pad_cols(_swap_halves(rope), LANES).reshape(MLA_Q_LORA, -1)], axis=1).astype(BF16)
    wkv3 = w_kvb.reshape(MLA_KV_LORA, MLA_HEADS, MLA_NOPE + MLA_V)
    wuk = jnp.transpose(wkv3[..., :MLA_NOPE], (1, 2, 0)).astype(BF16)
    wuv = jnp.transpose(wkv3[..., MLA_NOPE:], (1, 0, 2)).astype(BF16)
    return dict(wall=wall, cw=cw, dtb=dtb, alog=alog, dskip=dskip, snw=_row(ssd_norm_w), qnw=_row(q_norm_w), wq=wq,
                wuk=wuk, kvnw=_row(kv_norm_w), wuv=wuv, wo=w_out.astype(BF16))


def _rope_tables(seq):
    inv = 1.0 / (ROPE_THETA ** (jnp.arange(0, MLA_ROPE, 2, dtype=F32) / MLA_ROPE))
    ang = jnp.arange(seq, dtype=F32)[:, None] * inv[None, :]
    cos, sin = jnp.cos(ang), jnp.sin(ang)
    return (_pad_cols(jnp.concatenate([cos, cos], axis=1), LANES),
            _pad_cols(jnp.concatenate([-sin, sin], axis=1), LANES))


def _prep_gla(w_in, w_gk2, b_gk, norm_w, w_out):
    w = _pad_cols(w_in, 2 * GLA_KEY + 2 * GLA_VAL + LANES).astype(BF16)
    wgk = jnp.zeros((LANES, 2 * GLA_KEY), F32)
    for d in range(2):
        wgk = wgk.at[d * GLA_GATE_RANK:(d + 1) * GLA_GATE_RANK, d * GLA_KEY:(d + 1) * GLA_KEY].set(w_gk2[d])
    return dict(w=w, wgk=wgk.astype(BF16), bgk=_row(b_gk), gnw=_row(norm_w), wo=w_out.astype(BF16))


def _prep_ffn(w_in, conv_w, conv_b, w_out):
    nchunk = D_FF // FFN_COL_CHUNK

    def cols(w):
        return jnp.transpose(w.reshape(w.shape[0], nchunk, FFN_COL_CHUNK), (1, 0, 2))

    cw = jnp.concatenate([conv_w, conv_b[None, :], jnp.zeros((4, 2 * D_FF), F32)], axis=0).astype(F32)
    return dict(wg=cols(w_in[:, :D_FF]).astype(BF16), wu=cols(w_in[:, D_FF:]).astype(BF16),
                cwg=cols(cw[:, :D_FF]), cwu=cols(cw[:, D_FF:]),
                wo=w_out.astype(BF16))


def _tile(total, pref):
    return min(total, pref)


def _prepare(p):
    layers = []
    for layer in range(p['norm_w'].shape[0]):
        i = layer // 2
        if layer % 2 == 0:
            mixer = _prep_even(p['hyb_w_in'][i], p['ssd_conv_w'][i], p['ssd_conv_b'][i], p['ssd_a_log'][i],
                               p['ssd_dt_bias'][i], p['ssd_d'][i], p['ssd_norm_w'][i], p['mla_q_norm_w'][i],
                               p['mla_w_qb'][i], p['mla_kv_norm_w'][i], p['mla_w_kvb'][i], p['hyb_w_out'][i])
        else:
            mixer = _prep_gla(p['gla_w_in'][i], p['gla_w_gk2'][i], p['gla_b_gk'][i], p['gla_norm_w'][i],
                              p['gla_w_out'][i])
        ffn = _prep_ffn(p['ffn_w_in'][layer], p['ffn_conv_w'][layer], p['ffn_conv_b'][layer], p['ffn_w_out'][layer])
        layers.append((p['norm_w'][layer].astype(F32), mixer, ffn))
    return layers


def _trunk(x, layers, tiles=None):
    tiles = dict(dict(even_in=512, ssd=512, ssd_seqs=1, mla=256, tok=512, gla=256, gla_seqs=2, ffn=512), **(tiles or {}))
    bsz, seq, d = x.shape
    cos, sin = _rope_tables(seq)
    flat = lambda t: t.reshape(bsz * seq, t.shape[-1])
    for layer, (nw, e, f) in enumerate(layers):
        if layer % 2 == 0:
            z, xs, bc, dt, q, k = _even_in(x, nw[0:1], e['wall'], e['cw'], e['dtb'], e['qnw'], e['wq'], e['wuk'],
                                           e['kvnw'], cos, sin, _tile(seq, tiles['even_in']))
            yf, yb = _ssd_scan(xs, bc, dt, e['alog'], _tile(seq, tiles['ssd']), math.gcd(bsz, tiles['ssd_seqs']))
            om = _mla_attention(q, k, e['wuv'], _tile(seq, tiles['mla']))
            x = _even_out(flat(yf), flat(yb), flat(xs), flat(z), flat(om), flat(x), e['dskip'], e['snw'], e['wo'],
                          nw[1:2], _tile(bsz * seq, tiles['tok'])).reshape(bsz, seq, d)
        else:
            q, k, v, g, gk = _gla_in(flat(x), nw[0:1], e['w'], e['wgk'], e['bgk'], _tile(bsz * seq, tiles['tok']))
            r3 = lambda t: t.reshape(bsz, seq, t.shape[-1])
            of, ob = _gla_scan(r3(q), r3(k), r3(v), r3(gk), _tile(seq, tiles['gla']), math.gcd(bsz, tiles['gla_seqs']))
            x = _gla_out(flat(of), flat(ob), g, flat(x), e['gnw'], e['wo'], nw[1:2],
                         _tile(bsz * seq, tiles['tok'])).reshape(bsz, seq, d)
        x = _ffn(x, nw[2:3], f['wg'], f['wu'], f['cwg'], f['cwu'], f['wo'], nw[3:4], _tile(seq, tiles['ffn']))
    return x


def kernel(x_prompt, x_sample, hyb_w_in, ssd_conv_w, ssd_conv_b, ssd_a_log, ssd_dt_bias, ssd_d, ssd_norm_w, mla_q_norm_w, mla_w_qb, mla_kv_norm_w, mla_w_kvb, hyb_w_out, gla_w_in, gla_w_gk2, gla_b_gk, gla_norm_w, gla_w_out, ffn_w_in, ffn_conv_w, ffn_conv_b, ffn_w_out, norm_w):
    params = dict(hyb_w_in=hyb_w_in, ssd_conv_w=ssd_conv_w, ssd_conv_b=ssd_conv_b, ssd_a_log=ssd_a_log,
                  ssd_dt_bias=ssd_dt_bias, ssd_d=ssd_d, ssd_norm_w=ssd_norm_w,
                  mla_q_norm_w=mla_q_norm_w, mla_w_qb=mla_w_qb, mla_kv_norm_w=mla_kv_norm_w,
                  mla_w_kvb=mla_w_kvb, hyb_w_out=hyb_w_out,
                  gla_w_in=gla_w_in, gla_w_gk2=gla_w_gk2, gla_b_gk=gla_b_gk, gla_norm_w=gla_norm_w,
                  gla_w_out=gla_w_out, ffn_w_in=ffn_w_in, ffn_conv_w=ffn_conv_w, ffn_conv_b=ffn_conv_b,
                  ffn_w_out=ffn_w_out, norm_w=norm_w)
    layers = _prepare(params)
    return (_trunk(x_prompt, layers), _trunk(x_sample, layers))
```
